```python
import jax, jax.numpy as jnp
from jax import lax
import numpy as np

D_MODEL = 2048
BATCH = 8
SEQ = 2048
DEPTH = 2

N_HEADS = 8
N_KV_HEADS = 2
HEAD_DIM = 128
GQA = N_HEADS // N_KV_HEADS
ATTN_W = N_HEADS * HEAD_DIM
KV_W = N_KV_HEADS * HEAD_DIM
Q_BLOCK = 128
ROPE_THETA = 10000.0
GRID_W = 64
CONV_GROUPS = 8
CONV_W = D_MODEL - ATTN_W
CONV_GROUP_DIM = CONV_W // CONV_GROUPS
CONV_WIDTH = 3
IN_W = ATTN_W + 2 * KV_W + 3 * CONV_W
D_FF_DENSE = 5632
N_EXPERTS = 8
TOP_K = 2
D_FF_EXPERT = 4096
N_DENSE = (DEPTH + 1) // 2
N_MOE = DEPTH // 2
EPS = 1e-6

kernel_name = 'hybrid_attn_shortconv_adaln_moe_encoder'


def rms_norm(x, g):
    xf = x.astype(jnp.float32)
    y = xf * lax.rsqrt(jnp.mean(xf * xf, axis=-1, keepdims=True) + EPS)
    return (y * g.astype(jnp.float32)).astype(x.dtype)


def head_rms_norm(y, g, n_heads):
    b, s, w = y.shape
    hd = w // n_heads
    yh = rms_norm(y.reshape(b, s, n_heads, hd), jnp.ones((hd,), y.dtype))
    return yh.reshape(b, s, w) * g.astype(y.dtype)


def axial_rope_tables(seq):
    rows = seq // GRID_W
    row = jnp.repeat(jnp.arange(rows, dtype=jnp.float32), GRID_W)
    col = jnp.tile(jnp.arange(GRID_W, dtype=jnp.float32), rows)
    quarter = HEAD_DIM // 4
    inv = ROPE_THETA ** (-jnp.arange(quarter, dtype=jnp.float32) / quarter)
    ang_r = row[:, None] * inv
    ang_c = col[:, None] * inv
    ang = jnp.concatenate([ang_r, ang_r, ang_c, ang_c], axis=-1)
    return jnp.cos(ang), jnp.sin(ang)


def rotate_sections(x):
    xs = x.reshape(x.shape[:-1] + (2, 2, HEAD_DIM // 4))
    rot = jnp.stack([-xs[..., 1, :], xs[..., 0, :]], axis=-2)
    return rot.reshape(x.shape)


def apply_rope(x, cos, sin):
    xf = x.astype(jnp.float32)
    y = xf * cos[None, :, None, :] + rotate_sections(xf) * sin[None, :, None, :]
    return y.astype(x.dtype)


def attention_group(q, k, v, cos, sin, q_g, k_g):
    q = apply_rope(rms_norm(q, q_g), cos, sin)
    k = apply_rope(rms_norm(k, k_g), cos, sin)
    b, s = q.shape[:2]
    nblk = s // Q_BLOCK
    qb = q.reshape(b, nblk, Q_BLOCK, N_KV_HEADS, GQA, HEAD_DIM).transpose(1, 0, 2, 3, 4, 5)
    scale = HEAD_DIM ** -0.5

    def block(q_blk):
        sc = jnp.einsum('bqhgd,bkhd->bhgqk', q_blk, k, preferred_element_type=jnp.float32) * scale
        p = jax.nn.softmax(sc, axis=-1).astype(v.dtype)
        return jnp.einsum('bhgqk,bkhd->bqhgd', p, v)

    o = lax.map(block, qb)
    return o.transpose(1, 0, 2, 3, 4, 5).reshape(b, s, ATTN_W)


def short_conv_group(h, gate_b, gate_c, w):
    u = gate_c * h
    up = jnp.pad(u, ((0, 0), (1, 1), (0, 0)))
    y = up[:, :-2] * w[0] + up[:, 1:-1] * w[1] + up[:, 2:] * w[2]
    return gate_b * y


def swiglu(h, w_gate, w_up, w_down):
    return (jax.nn.silu(h @ w_gate) * (h @ w_up)) @ w_down


def moe_swiglu(h, router, w_gate, w_up, w_down):
    b, s, d = h.shape
    t = h.reshape(b * s, d)
    logits = jnp.dot(t, router, preferred_element_type=jnp.float32)
    probs = jax.nn.softmax(logits, axis=-1)
    top_v, top_i = lax.top_k(probs, TOP_K)
    top_v = top_v / jnp.sum(top_v, axis=-1, keepdims=True)
    combine = jnp.sum(jax.nn.one_hot(top_i, N_EXPERTS, dtype=jnp.float32) * top_v[..., None], axis=1).astype(h.dtype)
    out = jnp.zeros_like(t)
    for e in range(N_EXPERTS):
        out = out + combine[:, e:e + 1] * swiglu(t, w_gate[e], w_up[e], w_down[e])
    return out.reshape(b, s, d)


def setup_inputs(seed: int = 0) -> dict:
    key = jax.random.key(seed)
    ks = jax.random.split(key, 24)
    n = lambda k, shape, s: jax.random.normal(k, shape, jnp.float32) * s
    D = D_MODEL
    return {
        'x': n(ks[0], (BATCH, SEQ, D), 1.0),
        'c': n(ks[1], (BATCH, D), 1.0),
        'norm1_g': 1.0 + n(ks[2], (DEPTH, D), 0.02),
        'w_ada': n(ks[3], (DEPTH, D, 6 * D), 0.5 * D ** -0.5),
        'b_ada': n(ks[4], (DEPTH, 6 * D), 0.02),
        'w_in': n(ks[5], (DEPTH, D, IN_W), D ** -0.5),
        'q_norm_g': 1.0 + n(ks[6], (DEPTH, HEAD_DIM), 0.02),
        'k_norm_g': 1.0 + n(ks[7], (DEPTH, HEAD_DIM), 0.02),
        'conv_w': n(ks[8], (DEPTH, CONV_WIDTH, CONV_W), CONV_WIDTH ** -0.5),
        'attn_out_g': 1.0 + n(ks[9], (DEPTH, ATTN_W), 0.02),
        'conv_out_g': 1.0 + n(ks[10], (DEPTH, CONV_W), 0.02),
        'w_out': n(ks[11], (DEPTH, D, D), D ** -0.5),
        'norm2_g': 1.0 + n(ks[12], (DEPTH, D), 0.02),
        'dense_w_gate': n(ks[13], (N_DENSE, D, D_FF_DENSE), D ** -0.5),
        'dense_w_up': n(ks[14], (N_DENSE, D, D_FF_DENSE), D ** -0.5),
        'dense_w_down': n(ks[15], (N_DENSE, D_FF_DENSE, D), D_FF_DENSE ** -0.5),
        'moe_router': n(ks[16], (N_MOE, D, N_EXPERTS), D ** -0.5),
        'moe_w_gate': n(ks[17], (N_MOE, N_EXPERTS, D, D_FF_EXPERT), D ** -0.5),
        'moe_w_up': n(ks[18], (N_MOE, N_EXPERTS, D, D_FF_EXPERT), D ** -0.5),
        'moe_w_down': n(ks[19], (N_MOE, N_EXPERTS, D_FF_EXPERT, D), D_FF_EXPERT ** -0.5),
    }


def reference(x, c, norm1_g, w_ada, b_ada, w_in, q_norm_g, k_norm_g, conv_w, attn_out_g,
              conv_out_g, w_out, norm2_g, dense_w_gate, dense_w_up, dense_w_down,
              moe_router, moe_w_gate, moe_w_up, moe_w_down):
    b, s, _ = x.shape
    cos, sin = axial_rope_tables(s)
    splits = [ATTN_W, ATTN_W + KV_W, ATTN_W + 2 * KV_W,
              ATTN_W + 2 * KV_W + CONV_W, ATTN_W + 2 * KV_W + 2 * CONV_W]
    c_act = jax.nn.silu(c)
    for l in range(DEPTH):
        mod = (c_act @ w_ada[l] + b_ada[l])[:, None, :]
        sh1, sc1, g1, sh2, sc2, g2 = jnp.split(mod, 6, axis=-1)
        h = rms_norm(x, norm1_g[l]) * (1.0 + sc1) + sh1
        p = h @ w_in[l]
        q, k, v, hc, gb, gc = jnp.split(p, splits, axis=-1)
        attn = attention_group(q.reshape(b, s, N_HEADS, HEAD_DIM),
                               k.reshape(b, s, N_KV_HEADS, HEAD_DIM),
                               v.reshape(b, s, N_KV_HEADS, HEAD_DIM),
                               cos, sin, q_norm_g[l], k_norm_g[l])
        conv = short_conv_group(hc, gb, gc, conv_w[l])
        mixed = jnp.concatenate([head_rms_norm(attn, attn_out_g[l], N_HEADS),
                                 head_rms_norm(conv, conv_out_g[l], CONV_GROUPS)], axis=-1)
        x = x + g1 * (mixed @ w_out[l])
        h2 = rms_norm(x, norm2_g[l]) * (1.0 + sc2) + sh2
        if l % 2 == 0:
            i = l // 2
            f = swiglu(h2, dense_w_gate[i], dense_w_up[i], dense_w_down[i])
        else:
            i = l // 2
            f = moe_swiglu(h2, moe_router[i], moe_w_gate[i], moe_w_up[i], moe_w_down[i])
        x = x + g2 * f
    return x
```

```python
import functools

import jax
import jax.numpy as jnp
from jax import lax
from jax.experimental import pallas as pl
from jax.experimental.pallas import tpu as pltpu

F32 = jnp.float32
BF16 = jnp.bfloat16

N_HEADS = 8
N_KV_HEADS = 2
HEAD_DIM = 128
GQA = N_HEADS // N_KV_HEADS
ATTN_W = N_HEADS * HEAD_DIM
KV_W = N_KV_HEADS * HEAD_DIM
ROPE_THETA = 10000.0
GRID_W = 64
CONV_GROUPS = 8
N_EXPERTS = 8
EPS = 1e-6

V7X_VMEM_BYTES = 64 * 1024 * 1024
VMEM_LIMIT = V7X_VMEM_BYTES - 8 * 1024 * 1024

ROW_TILE = 1024
Q_TILE = 512
FF_TILE = 512
IN_TILE = 1536
OUT_TILE = 1024
ADA_TILE = 1024


def _params(*sem):
    return pltpu.CompilerParams(dimension_semantics=sem, vmem_limit_bytes=VMEM_LIMIT)


def _sigmoid(x):
    return 1.0 / (1.0 + jnp.exp(-x))


def _modulated_norm(x, gain, shift, scale):
    ms = jnp.mean(x * x, axis=-1, keepdims=True)
    return (x * lax.rsqrt(ms + EPS) * gain) * (1.0 + scale) + shift


def _ada_kernel(c_ref, w_ref, b_ref, o_ref):
    c = c_ref[...]
    act = (c * _sigmoid(c)).astype(BF16)
    o_ref[...] = jnp.dot(act, w_ref[...].astype(BF16), preferred_element_type=F32) + b_ref[...]


def _ada_mod(c, w_ada, b_ada):
    depth, d, n = w_ada.shape
    b = c.shape[0]
    out = pl.pallas_call(
        _ada_kernel,
        out_shape=jax.ShapeDtypeStruct((depth, b, n), F32),
        grid=(depth, n // ADA_TILE),
        in_specs=[
            pl.BlockSpec((b, d), lambda l, j: (0, 0)),
            pl.BlockSpec((None, d, ADA_TILE), lambda l, j: (l, 0, j)),
            pl.BlockSpec((None, 1, ADA_TILE), lambda l, j: (l, 0, j)),
        ],
        out_specs=pl.BlockSpec((None, b, ADA_TILE), lambda l, j: (l, 0, j)),
        compiler_params=_params("parallel", "parallel"),
        name="ada_mod",
    )(c, w_ada, b_ada.reshape(depth, 1, n))
    return out.reshape(depth, b, 6, d)


def _inproj_kernel(x_ref, mod_ref, g_ref, w_ref, o_ref, h_ref):
    @pl.when(pl.program_id(1) == 0)
    def _():
        h = _modulated_norm(x_ref[...], g_ref[...], mod_ref[0:1, :], mod_ref[1:2, :])
        h_ref[...] = h.astype(BF16)

    o_ref[...] = jnp.dot(h_ref[...], w_ref[...], preferred_element_type=F32).astype(o_ref.dtype)


def _in_projection(x, mod, gain, w, seq):
    t, d = x.shape
    n = w.shape[1]
    tiles_per_seq = seq // ROW_TILE
    return pl.pallas_call(
        _inproj_kernel,
        out_shape=jax.ShapeDtypeStruct((t, n), BF16),
        grid=(t // ROW_TILE, n // IN_TILE),
        in_specs=[
            pl.BlockSpec((ROW_TILE, d), lambda i, j: (i, 0)),
            pl.BlockSpec((None, 6, d), lambda i, j: (i // tiles_per_seq, 0, 0)),
            pl.BlockSpec((1, d), lambda i, j: (0, 0)),
            pl.BlockSpec((d, IN_TILE), lambda i, j: (0, j)),
        ],
        out_specs=pl.BlockSpec((ROW_TILE, IN_TILE), lambda i, j: (i, j)),
        scratch_shapes=[pltpu.VMEM((ROW_TILE, d), BF16)],
        compiler_params=_params("parallel", "arbitrary"),
        name="in_projection",
    )(x, mod, gain, w)


def _rope_tables(seq):
    pos = jnp.arange(seq, dtype=jnp.int32)
    row = (pos // GRID_W).astype(F32)
    col = (pos % GRID_W).astype(F32)
    quarter = HEAD_DIM // 4
    inv = ROPE_THETA ** (-jnp.arange(quarter, dtype=F32) / quarter)
    ang_r = row[:, None] * inv
    ang_c = col[:, None] * inv
    ang = jnp.concatenate([ang_r, ang_r, ang_c, ang_c], axis=-1)
    cos, sin = jnp.cos(ang), jnp.sin(ang)
    first_half = (jnp.arange(HEAD_DIM) % (2 * quarter)) < quarter
    sin_up = jnp.where(first_half, -sin, 0.0)
    sin_dn = jnp.where(first_half, 0.0, sin)
    return cos, sin_up, sin_dn


def _norm_rope(x, gain, cos, sin_up, sin_dn):
    ms = jnp.mean(x * x, axis=-1, keepdims=True)
    y = x * lax.rsqrt(ms + EPS) * gain
    quarter = HEAD_DIM // 4
    return (y * cos + pltpu.roll(y, HEAD_DIM - quarter, 1) * sin_up
            + pltpu.roll(y, quarter, 1) * sin_dn)


def _attn_kernel(q_ref, k_ref, v_ref, cos_ref, su_ref, sd_ref, qg_ref, kg_ref, og_ref,
                 o_ref, kt_ref):
    qt = pl.program_id(2)

    @pl.when(qt == 0)
    def _():
        k = _norm_rope(k_ref[...].astype(F32), kg_ref[...], cos_ref[...], su_ref[...], sd_ref[...])
        kt_ref[...] = k.T.astype(BF16)

    r0 = pl.multiple_of(qt * Q_TILE, Q_TILE)
    cos = cos_ref[pl.ds(r0, Q_TILE), :]
    s_up = su_ref[pl.ds(r0, Q_TILE), :]
    s_dn = sd_ref[pl.ds(r0, Q_TILE), :]
    scale = HEAD_DIM ** -0.5
    for g in range(GQA):
        cols = slice(g * HEAD_DIM, (g + 1) * HEAD_DIM)
        q = _norm_rope(q_ref[:, cols].astype(F32), qg_ref[...], cos, s_up, s_dn) * scale
        s = jnp.dot(q.astype(BF16), kt_ref[...], preferred_element_type=F32)
        p = jnp.exp(s - jnp.max(s, axis=-1, keepdims=True))
        denom = jnp.sum(p, axis=-1, keepdims=True)
        o = jnp.dot(p.astype(BF16), v_ref[...], preferred_element_type=F32) / denom
        ms = jnp.mean(o * o, axis=-1, keepdims=True)
        o_ref[:, cols] = (o * lax.rsqrt(ms + EPS) * og_ref[:, cols]).astype(o_ref.dtype)


def _attention(p, tables, q_gain, k_gain, out_gain, batch, seq):
    t = p.shape[0]
    q_tiles = seq // Q_TILE
    group_w = GQA * HEAD_DIM
    k_blk0 = ATTN_W // HEAD_DIM
    v_blk0 = (ATTN_W + KV_W) // HEAD_DIM
    table_spec = pl.BlockSpec((seq, HEAD_DIM), lambda b, h, i: (0, 0))
    gain_spec = pl.BlockSpec((1, HEAD_DIM), lambda b, h, i: (0, 0))
    return pl.pallas_call(
        _attn_kernel,
        out_shape=jax.ShapeDtypeStruct((t, ATTN_W), BF16),
        grid=(batch, N_KV_HEADS, q_tiles),
        in_specs=[
            pl.BlockSpec((Q_TILE, group_w), lambda b, h, i: (b * q_tiles + i, h)),
            pl.BlockSpec((seq, HEAD_DIM), lambda b, h, i: (b, k_blk0 + h)),
            pl.BlockSpec((seq, HEAD_DIM), lambda b, h, i: (b, v_blk0 + h)),
            table_spec, table_spec, table_spec,
            gain_spec, gain_spec,
            pl.BlockSpec((1, group_w), lambda b, h, i: (0, h)),
        ],
        out_specs=pl.BlockSpec((Q_TILE, group_w), lambda b, h, i: (b * q_tiles + i, h)),
        scratch_shapes=[pltpu.VMEM((HEAD_DIM, seq), BF16)],
        compiler_params=_params("parallel", "parallel", "arbitrary"),
        name="attention",
    )(p, p, p, *tables, q_gain, k_gain, out_gain)


def _conv_kernel(h_ref, b_ref, c_ref, w_ref, g_ref, o_ref):
    u = c_ref[...].astype(F32) * h_ref[...].astype(F32)
    seq = u.shape[0]
    pos = lax.broadcasted_iota(jnp.int32, u.shape, 0)
    prev = jnp.where(pos == 0, 0.0, pltpu.roll(u, 1, 0))
    nxt = jnp.where(pos == seq - 1, 0.0, pltpu.roll(u, seq - 1, 0))
    y = prev * w_ref[0:1, :] + u * w_ref[1:2, :] + nxt * w_ref[2:3, :]
    y = b_ref[...].astype(F32) * y
    ms = jnp.mean(y * y, axis=-1, keepdims=True)
    o_ref[...] = (y * lax.rsqrt(ms + EPS) * g_ref[...]).astype(o_ref.dtype)


def _short_conv(p, conv_w, out_gain, batch, seq):
    t = p.shape[0]
    conv_w_total = conv_w.shape[1]
    gd = conv_w_total // CONV_GROUPS
    h0 = (ATTN_W + 2 * KV_W) // gd
    b0 = h0 + CONV_GROUPS
    c0 = b0 + CONV_GROUPS
    return pl.pallas_call(
        _conv_kernel,
        out_shape=jax.ShapeDtypeStruct((t, conv_w_total), BF16),
        grid=(batch, CONV_GROUPS),
        in_specs=[
            pl.BlockSpec((seq, gd), lambda b, g: (b, h0 + g)),
            pl.BlockSpec((seq, gd), lambda b, g: (b, b0 + g)),
            pl.BlockSpec((seq, gd), lambda b, g: (b, c0 + g)),
            pl.BlockSpec((3, gd), lambda b, g: (0, g)),
            pl.BlockSpec((1, gd), lambda b, g: (0, g)),
        ],
        out_specs=pl.BlockSpec((seq, gd), lambda b, g: (b, g)),
        compiler_params=_params("parallel", "parallel"),
        name="short_conv",
    )(p, p, p, conv_w, out_gain)


def _outproj_kernel(a_ref, c_ref, wa_ref, wc_ref, x_ref, mod_ref, o_ref):
    acc = jnp.dot(a_ref[...], wa_ref[...], preferred_element_type=F32)
    acc = acc + jnp.dot(c_ref[...], wc_ref[...], preferred_element_type=F32)
    o_ref[...] = x_ref[...] + mod_ref[2:3, :] * acc


def _out_projection(attn, conv, w, x, mod, seq):
    t, d = x.shape
    wa = attn.shape[1]
    wc = conv.shape[1]
    assert wa == wc
    tiles_per_seq = seq // ROW_TILE
    return pl.pallas_call(
        _outproj_kernel,
        out_shape=jax.ShapeDtypeStruct((t, d), F32),
        grid=(t // ROW_TILE, d // OUT_TILE),
        in_specs=[
            pl.BlockSpec((ROW_TILE, wa), lambda i, j: (i, 0)),
            pl.BlockSpec((ROW_TILE, wc), lambda i, j: (i, 0)),
            pl.BlockSpec((wa, OUT_TILE), lambda i, j: (0, j)),
            pl.BlockSpec((wc, OUT_TILE), lambda i, j: (1, j)),
            pl.BlockSpec((ROW_TILE, OUT_TILE), lambda i, j: (i, j)),
            pl.BlockSpec((None, 6, OUT_TILE), lambda i, j: (i // tiles_per_seq, 0, j)),
        ],
        out_specs=pl.BlockSpec((ROW_TILE, OUT_TILE), lambda i, j: (i, j)),
        compiler_params=_params("parallel", "parallel"),
        name="out_projection",
    )(attn, conv, w, w, x, mod)


def _swiglu_tile(h, wg, wu, wd, row_scale=None):
    g = jnp.dot(h, wg, preferred_element_type=F32)
    u = jnp.dot(h, wu, preferred_element_type=F32)
    a = g * _sigmoid(g) * u
    if row_scale is not None:
        a = a * row_scale
    return jnp.dot(a.astype(BF16), wd, preferred_element_type=F32)


def _ffn_kernel(x_ref, mod_ref, g_ref, wg_ref, wu_ref, wd_ref, o_ref, h_ref):
    f = pl.program_id(1)

    @pl.when(f == 0)
    def _():
        h = _modulated_norm(x_ref[...], g_ref[...], mod_ref[3:4, :], mod_ref[4:5, :])
        h_ref[...] = h.astype(BF16)
        o_ref[...] = jnp.zeros_like(o_ref)

    o_ref[...] += _swiglu_tile(h_ref[...], wg_ref[...], wu_ref[...], wd_ref[...])

    @pl.when(f == pl.num_programs(1) - 1)
    def _():
        o_ref[...] = x_ref[...] + mod_ref[5:6, :] * o_ref[...]


def _dense_ffn(x, mod, gain, wg, wu, wd, seq):
    t, d = x.shape
    ff = wg.shape[1]
    tiles_per_seq = seq // ROW_TILE
    return pl.pallas_call(
        _ffn_kernel,
        out_shape=jax.ShapeDtypeStruct((t, d), F32),
        grid=(t // ROW_TILE, ff // FF_TILE),
        in_specs=[
            pl.BlockSpec((ROW_TILE, d), lambda i, f: (i, 0), pipeline_mode=pl.Buffered(1)),
            pl.BlockSpec((None, 6, d), lambda i, f: (i // tiles_per_seq, 0, 0)),
            pl.BlockSpec((1, d), lambda i, f: (0, 0)),
            pl.BlockSpec((d, FF_TILE), lambda i, f: (0, f)),
            pl.BlockSpec((d, FF_TILE), lambda i, f: (0, f)),
            pl.BlockSpec((FF_TILE, d), lambda i, f: (f, 0)),
        ],
        out_specs=pl.BlockSpec((ROW_TILE, d), lambda i, f: (i, 0)),
        scratch_shapes=[pltpu.VMEM((ROW_TILE, d), BF16)],
        compiler_params=_params("parallel", "arbitrary"),
        name="dense_ffn",
    )(x, mod, gain, wg, wu, wd)


def _router_kernel(x_ref, mod_ref, g_ref, r_ref, o_ref):
    h = _modulated_norm(x_ref[...], g_ref[...], mod_ref[3:4, :], mod_ref[4:5, :])
    logits = jnp.dot(h, r_ref[...], preferred_element_type=F32, precision=lax.Precision.HIGHEST)
    z = jnp.exp(logits - jnp.max(logits, axis=-1, keepdims=True))
    probs = z / jnp.sum(z, axis=-1, keepdims=True)
    lane = lax.broadcasted_iota(jnp.int32, probs.shape, 1)
    n = probs.shape[1]
    v1 = jnp.max(probs, axis=-1, keepdims=True)
    i1 = jnp.min(jnp.where(probs == v1, lane, n), axis=-1, keepdims=True)
    rest = jnp.where(lane == i1, -1.0, probs)
    v2 = jnp.max(rest, axis=-1, keepdims=True)
    i2 = jnp.min(jnp.where(rest == v2, lane, n), axis=-1, keepdims=True)
    total = v1 + v2
    o_ref[...] = jnp.where(lane == i1, v1 / total, 0.0) + jnp.where(lane == i2, v2 / total, 0.0)


def _router(x, mod, gain, router, seq):
    t, d = x.shape
    n = router.shape[1]
    tiles_per_seq = seq // ROW_TILE
    return pl.pallas_call(
        _router_kernel,
        out_shape=jax.ShapeDtypeStruct((t, n), F32),
        grid=(t // ROW_TILE,),
        in_specs=[
            pl.BlockSpec((ROW_TILE, d), lambda i: (i, 0)),
            pl.BlockSpec((None, 6, d), lambda i: (i // tiles_per_seq, 0, 0)),
            pl.BlockSpec((1, d), lambda i: (0, 0)),
            pl.BlockSpec((d, n), lambda i: (0, 0)),
        ],
        out_specs=pl.BlockSpec((ROW_TILE, n), lambda i: (i, 0)),
        compiler_params=_params("parallel"),
        name="router",
    )(x, mod, gain, router)


def _moe_kernel(x_ref, mod_ref, g_ref, cmb_ref, wg_ref, wu_ref, wd_ref, o_ref, h_ref):
    e = pl.program_id(1)
    f = pl.program_id(2)

    @pl.when((e == 0) & (f == 0))
    def _():
        h = _modulated_norm(x_ref[...], g_ref[...], mod_ref[3:4, :], mod_ref[4:5, :])
        h_ref[...] = h.astype(BF16)
        o_ref[...] = jnp.zeros_like(o_ref)

    cmb = cmb_ref[...]
    lane = lax.broadcasted_iota(jnp.int32, cmb.shape, 1)
    weight = jnp.sum(jnp.where(lane == e, cmb, 0.0), axis=-1, keepdims=True)
    o_ref[...] += _swiglu_tile(h_ref[...], wg_ref[...], wu_ref[...], wd_ref[...], weight)

    @pl.when((e == pl.num_programs(1) - 1) & (f == pl.num_programs(2) - 1))
    def _():
        o_ref[...] = x_ref[...] + mod_ref[5:6, :] * o_ref[...]


def _moe_ffn(x, mod, gain, combine, wg, wu, wd, seq):
    t, d = x.shape
    n_exp, _, ff = wg.shape
    tiles_per_seq = seq // ROW_TILE
    return pl.pallas_call(
        _moe_kernel,
        out_shape=jax.ShapeDtypeStruct((t, d), F32),
        grid=(t // ROW_TILE, n_exp, ff // FF_TILE),
        in_specs=[
            pl.BlockSpec((ROW_TILE, d), lambda i, e, f: (i, 0), pipeline_mode=pl.Buffered(1)),
            pl.BlockSpec((None, 6, d), lambda i, e, f: (i // tiles_per_seq, 0, 0)),
            pl.BlockSpec((1, d), lambda i, e, f: (0, 0)),
            pl.BlockSpec((ROW_TILE, n_exp), lambda i, e, f: (i, 0)),
            pl.BlockSpec((None, d, FF_TILE), lambda i, e, f: (e, 0, f)),
            pl.BlockSpec((None, d, FF_TILE), lambda i, e, f: (e, 0, f)),
            pl.BlockSpec((None, FF_TILE, d), lambda i, e, f: (e, f, 0)),
        ],
        out_specs=pl.BlockSpec((ROW_TILE, d), lambda i, e, f: (i, 0)),
        scratch_shapes=[pltpu.VMEM((ROW_TILE, d), BF16)],
        compiler_params=_params("parallel", "arbitrary", "arbitrary"),
        name="moe_ffn",
    )(x, mod, gain, combine, wg, wu, wd)


def kernel(x, c, norm1_g, w_ada, b_ada, w_in, q_norm_g, k_norm_g, conv_w, attn_out_g, conv_out_g,
           w_out, norm2_g, dense_w_gate, dense_w_up, dense_w_down, moe_router, moe_w_gate,
           moe_w_up, moe_w_down):
    batch, seq, d = x.shape
    depth = w_in.shape[0]
    assert seq % ROW_TILE == 0 and seq % Q_TILE == 0
    tables = _rope_tables(seq)
    mod = _ada_mod(c, w_ada, b_ada)
    xt = x.reshape(batch * seq, d)
    for l in range(depth):
        p = _in_projection(xt, mod[l], norm1_g[l][None], w_in[l].astype(BF16), seq)
        attn = _attention(p, tables, q_norm_g[l][None], k_norm_g[l][None], attn_out_g[l][None],
                          batch, seq)
        conv = _short_conv(p, conv_w[l], conv_out_g[l][None], batch, seq)
        xt = _out_projection(attn, conv, w_out[l].astype(BF16), xt, mod[l], seq)
        i = l // 2
        if l % 2 == 0:
            xt = _dense_ffn(xt, mod[l], norm2_g[l][None], dense_w_gate[i].astype(BF16),
                            dense_w_up[i].astype(BF16), dense_w_down[i].astype(BF16), seq)
        else:
            combine = _router(xt, mod[l], norm2_g[l][None], moe_router[i], seq)
            xt = _moe_ffn(xt, mod[l], norm2_g[l][None], combine, moe_w_gate[i].astype(BF16),
                          moe_w_up[i].astype(BF16), moe_w_down[i].astype(BF16), seq)
    return xt.reshape(batch, seq, d)
```

```python
import functools

import jax
import jax.numpy as jnp
from jax import lax
from jax.experimental import pallas as pl
from jax.experimental.pallas import tpu as pltpu

F32 = jnp.float32
BF16 = jnp.bfloat16

N_HEADS = 8
N_KV_HEADS = 2
HEAD_DIM = 128
GQA = N_HEADS // N_KV_HEADS
ATTN_W = N_HEADS * HEAD_DIM
KV_W = N_KV_HEADS * HEAD_DIM
ROPE_THETA = 10000.0
GRID_W = 64
CONV_GROUPS = 8
N_EXPERTS = 8
EPS = 1e-6

V7X_VMEM_BYTES = 64 * 1024 * 1024
VMEM_LIMIT = V7X_VMEM_BYTES - 8 * 1024 * 1024

ROW_TILE = 1024
Q_TILE = 512
FF_TILE = 512
IN_TILE = 1536
OUT_TILE = 1024
ADA_TILE = 1024
MOE_TILE = 512
MOE_FF_TILE = 1024
COMBINE_TILE = 512


def _params(*sem):
    return pltpu.CompilerParams(dimension_semantics=sem, vmem_limit_bytes=VMEM_LIMIT)


def _sigmoid(x):
    return 1.0 / (1.0 + jnp.exp(-x))


def _modulated_norm(x, gain, shift, scale):
    ms = jnp.mean(x * x, axis=-1, keepdims=True)
    return (x * lax.rsqrt(ms + EPS) * gain) * (1.0 + scale) + shift


def _ada_kernel(c_ref, w_ref, b_ref, o_ref):
    c = c_ref[...]
    act = (c * _sigmoid(c)).astype(BF16)
    o_ref[...] = jnp.dot(act, w_ref[...].astype(BF16), preferred_element_type=F32) + b_ref[...]


def _ada_mod(c, w_ada, b_ada):
    depth, d, n = w_ada.shape
    b = c.shape[0]
    out = pl.pallas_call(
        _ada_kernel,
        out_shape=jax.ShapeDtypeStruct((depth, b, n), F32),
        grid=(depth, n // ADA_TILE),
        in_specs=[
            pl.BlockSpec((b, d), lambda l, j: (0, 0)),
            pl.BlockSpec((None, d, ADA_TILE), lambda l, j: (l, 0, j)),
            pl.BlockSpec((None, 1, ADA_TILE), lambda l, j: (l, 0, j)),
        ],
        out_specs=pl.BlockSpec((None, b, ADA_TILE), lambda l, j: (l, 0, j)),
        compiler_params=_params("parallel", "parallel"),
        name="ada_mod",
    )(c, w_ada, b_ada.reshape(depth, 1, n))
    return out.reshape(depth, b, 6, d)


def _inproj_kernel(x_ref, mod_ref, g_ref, w_ref, o_ref, h_ref):
    @pl.when(pl.program_id(1) == 0)
    def _():
        h = _modulated_norm(x_ref[...], g_ref[...], mod_ref[0:1, :], mod_ref[1:2, :])
        h_ref[...] = h.astype(BF16)

    o_ref[...] = jnp.dot(h_ref[...], w_ref[...], preferred_element_type=F32).astype(o_ref.dtype)


def _in_projection(x, mod, gain, w, seq):
    t, d = x.shape
    n = w.shape[1]
    tiles_per_seq = seq // ROW_TILE
    return pl.pallas_call(
        _inproj_kernel,
        out_shape=jax.ShapeDtypeStruct((t, n), BF16),
        grid=(t // ROW_TILE, n // IN_TILE),
        in_specs=[
            pl.BlockSpec((ROW_TILE, d), lambda i, j: (i, 0)),
            pl.BlockSpec((None, 6, d), lambda i, j: (i // tiles_per_seq, 0, 0)),
            pl.BlockSpec((1, d), lambda i, j: (0, 0)),
            pl.BlockSpec((d, IN_TILE), lambda i, j: (0, j)),
        ],
        out_specs=pl.BlockSpec((ROW_TILE, IN_TILE), lambda i, j: (i, j)),
        scratch_shapes=[pltpu.VMEM((ROW_TILE, d), BF16)],
        compiler_params=_params("parallel", "arbitrary"),
        name="in_projection",
    )(x, mod, gain, w)


def _rope_tables(seq):
    pos = jnp.arange(seq, dtype=jnp.int32)
    row = (pos // GRID_W).astype(F32)
    col = (pos % GRID_W).astype(F32)
    quarter = HEAD_DIM // 4
    inv = ROPE_THETA ** (-jnp.arange(quarter, dtype=F32) / quarter)
    ang_r = row[:, None] * inv
    ang_c = col[:, None] * inv
    ang = jnp.concatenate([ang_r, ang_r, ang_c, ang_c], axis=-1)
    cos, sin = jnp.cos(ang), jnp.sin(ang)
    first_half = (jnp.arange(HEAD_DIM) % (2 * quarter)) < quarter
    sin_up = jnp.where(first_half, -sin, 0.0)
    sin_dn = jnp.where(first_half, 0.0, sin)
    return cos, sin_up, sin_dn


def _norm_rope(x, gain, cos, sin_up, sin_dn):
    ms = jnp.mean(x * x, axis=-1, keepdims=True)
    y = x * lax.rsqrt(ms + EPS) * gain
    quarter = HEAD_DIM // 4
    return (y * cos + pltpu.roll(y, HEAD_DIM - quarter, 1) * sin_up
            + pltpu.roll(y, quarter, 1) * sin_dn)


def _attn_kernel(q_ref, k_ref, v_ref, cos_ref, su_ref, sd_ref, qg_ref, kg_ref, og_ref,
                 o_ref, kt_ref):
    qt = pl.program_id(2)

    @pl.when(qt == 0)
    def _():
        k = _norm_rope(k_ref[...].astype(F32), kg_ref[...], cos_ref[...], su_ref[...], sd_ref[...])
        kt_ref[...] = k.T.astype(BF16)

    r0 = pl.multiple_of(qt * Q_TILE, Q_TILE)
    cos = cos_ref[pl.ds(r0, Q_TILE), :]
    s_up = su_ref[pl.ds(r0, Q_TILE), :]
    s_dn = sd_ref[pl.ds(r0, Q_TILE), :]
    scale = HEAD_DIM ** -0.5
    for g in range(GQA):
        cols = slice(g * HEAD_DIM, (g + 1) * HEAD_DIM)
        q = _norm_rope(q_ref[:, cols].astype(F32), qg_ref[...], cos, s_up, s_dn) * scale
        s = jnp.dot(q.astype(BF16), kt_ref[...], preferred_element_type=F32)
        p = jnp.exp(s - jnp.max(s, axis=-1, keepdims=True))
        denom = jnp.sum(p, axis=-1, keepdims=True)
        o = jnp.dot(p.astype(BF16), v_ref[...], preferred_element_type=F32) / denom
        ms = jnp.mean(o * o, axis=-1, keepdims=True)
        o_ref[:, cols] = (o * lax.rsqrt(ms + EPS) * og_ref[:, cols]).astype(o_ref.dtype)


def _attention(p, tables, q_gain, k_gain, out_gain, batch, seq):
    t = p.shape[0]
    q_tiles = seq // Q_TILE
    group_w = GQA * HEAD_DIM
    k_blk0 = ATTN_W // HEAD_DIM
    v_blk0 = (ATTN_W + KV_W) // HEAD_DIM
    table_spec = pl.BlockSpec((seq, HEAD_DIM), lambda b, h, i: (0, 0))
    gain_spec = pl.BlockSpec((1, HEAD_DIM), lambda b, h, i: (0, 0))
    return pl.pallas_call(
        _attn_kernel,
        out_shape=jax.ShapeDtypeStruct((t, ATTN_W), BF16),
        grid=(batch, N_KV_HEADS, q_tiles),
        in_specs=[
            pl.BlockSpec((Q_TILE, group_w), lambda b, h, i: (b * q_tiles + i, h)),
            pl.BlockSpec((seq, HEAD_DIM), lambda b, h, i: (b, k_blk0 + h)),
            pl.BlockSpec((seq, HEAD_DIM), lambda b, h, i: (b, v_blk0 + h)),
            table_spec, table_spec, table_spec,
            gain_spec, gain_spec,
            pl.BlockSpec((1, group_w), lambda b, h, i: (0, h)),
        ],
        out_specs=pl.BlockSpec((Q_TILE, group_w), lambda b, h, i: (b * q_tiles + i, h)),
        scratch_shapes=[pltpu.VMEM((HEAD_DIM, seq), BF16)],
        compiler_params=_params("parallel", "parallel", "arbitrary"),
        name="attention",
    )(p, p, p, *tables, q_gain, k_gain, out_gain)


def _conv_kernel(h_ref, b_ref, c_ref, w_ref, g_ref, o_ref):
    u = c_ref[...].astype(F32) * h_ref[...].astype(F32)
    seq = u.shape[0]
    pos = lax.broadcasted_iota(jnp.int32, u.shape, 0)
    prev = jnp.where(pos == 0, 0.0, pltpu.roll(u, 1, 0))
    nxt = jnp.where(pos == seq - 1, 0.0, pltpu.roll(u, seq - 1, 0))
    y = prev * w_ref[0:1, :] + u * w_ref[1:2, :] + nxt * w_ref[2:3, :]
    y = b_ref[...].astype(F32) * y
    ms = jnp.mean(y * y, axis=-1, keepdims=True)
    o_ref[...] = (y * lax.rsqrt(ms + EPS) * g_ref[...]).astype(o_ref.dtype)


def _short_conv(p, conv_w, out_gain, batch, seq):
    t = p.shape[0]
    conv_w_total = conv_w.shape[1]
    gd = conv_w_total // CONV_GROUPS
    h0 = (ATTN_W + 2 * KV_W) // gd
    b0 = h0 + CONV_GROUPS
    c0 = b0 + CONV_GROUPS
    return pl.pallas_call(
        _conv_kernel,
        out_shape=jax.ShapeDtypeStruct((t, conv_w_total), BF16),
        grid=(batch, CONV_GROUPS),
        in_specs=[
            pl.BlockSpec((seq, gd), lambda b, g: (b, h0 + g)),
            pl.BlockSpec((seq, gd), lambda b, g: (b, b0 + g)),
            pl.BlockSpec((seq, gd), lambda b, g: (b, c0 + g)),
            pl.BlockSpec((3, gd), lambda b, g: (0, g)),
            pl.BlockSpec((1, gd), lambda b, g: (0, g)),
        ],
        out_specs=pl.BlockSpec((seq, gd), lambda b, g: (b, g)),
        compiler_params=_params("parallel", "parallel"),
        name="short_conv",
    )(p, p, p, conv_w, out_gain)


def _outproj_kernel(a_ref, c_ref, wa_ref, wc_ref, x_ref, mod_ref, o_ref):
    acc = jnp.dot(a_ref[...], wa_ref[...], preferred_element_type=F32)
    acc = acc + jnp.dot(c_ref[...], wc_ref[...], preferred_element_type=F32)
    o_ref[...] = x_ref[...] + mod_ref[2:3, :] * acc


def _out_projection(attn, conv, w, x, mod, seq):
    t, d = x.shape
    wa = attn.shape[1]
    wc = conv.shape[1]
    assert wa == wc
    tiles_per_seq = seq // ROW_TILE
    return pl.pallas_call(
        _outproj_kernel,
        out_shape=jax.ShapeDtypeStruct((t, d), F32),
        grid=(t // ROW_TILE, d // OUT_TILE),
        in_specs=[
            pl.BlockSpec((ROW_TILE, wa), lambda i, j: (i, 0)),
            pl.BlockSpec((ROW_TILE, wc), lambda i, j: (i, 0)),
            pl.BlockSpec((wa, OUT_TILE), lambda i, j: (0, j)),
            pl.BlockSpec((wc, OUT_TILE), lambda i, j: (1, j)),
            pl.BlockSpec((ROW_TILE, OUT_TILE), lambda i, j: (i, j)),
            pl.BlockSpec((None, 6, OUT_TILE), lambda i, j: (i // tiles_per_seq, 0, j)),
        ],
        out_specs=pl.BlockSpec((ROW_TILE, OUT_TILE), lambda i, j: (i, j)),
        compiler_params=_params("parallel", "parallel"),
        name="out_projection",
    )(attn, conv, w, w, x, mod)


def _swiglu_tile(h, wg, wu, wd, row_scale=None):
    g = jnp.dot(h, wg, preferred_element_type=F32)
    u = jnp.dot(h, wu, preferred_element_type=F32)
    a = g * _sigmoid(g) * u
    if row_scale is not None:
        a = a * row_scale
    return jnp.dot(a.astype(BF16), wd, preferred_element_type=F32)


def _ffn_kernel(x_ref, mod_ref, g_ref, wg_ref, wu_ref, wd_ref, o_ref, h_ref):
    f = pl.program_id(1)

    @pl.when(f == 0)
    def _():
        h = _modulated_norm(x_ref[...], g_ref[...], mod_ref[3:4, :], mod_ref[4:5, :])
        h_ref[...] = h.astype(BF16)
        o_ref[...] = jnp.zeros_like(o_ref)

    o_ref[...] += _swiglu_tile(h_ref[...], wg_ref[...], wu_ref[...], wd_ref[...])

    @pl.when(f == pl.num_programs(1) - 1)
    def _():
        o_ref[...] = x_ref[...] + mod_ref[5:6, :] * o_ref[...]


def _dense_ffn(x, mod, gain, wg, wu, wd, seq):
    t, d = x.shape
    ff = wg.shape[1]
    tiles_per_seq = seq // ROW_TILE
    return pl.pallas_call(
        _ffn_kernel,
        out_shape=jax.ShapeDtypeStruct((t, d), F32),
        grid=(t // ROW_TILE, ff // FF_TILE),
        in_specs=[
            pl.BlockSpec((ROW_TILE, d), lambda i, f: (i, 0), pipeline_mode=pl.Buffered(1)),
            pl.BlockSpec((None, 6, d), lambda i, f: (i // tiles_per_seq, 0, 0)),
            pl.BlockSpec((1, d), lambda i, f: (0, 0)),
            pl.BlockSpec((d, FF_TILE), lambda i, f: (0, f)),
            pl.BlockSpec((d, FF_TILE), lambda i, f: (0, f)),
            pl.BlockSpec((FF_TILE, d), lambda i, f: (f, 0)),
        ],
        out_specs=pl.BlockSpec((ROW_TILE, d), lambda i, f: (i, 0)),
        scratch_shapes=[pltpu.VMEM((ROW_TILE, d), BF16)],
        compiler_params=_params("parallel", "arbitrary"),
        name="dense_ffn",
    )(x, mod, gain, wg, wu, wd)


def _router_kernel(x_ref, mod_ref, g_ref, r_ref, h_ref, idx_ref, wgt_ref):
    h = _modulated_norm(x_ref[...], g_ref[...], mod_ref[3:4, :], mod_ref[4:5, :])
    h_ref[...] = h
    logits = jnp.dot(h, r_ref[...], preferred_element_type=F32, precision=lax.Precision.HIGHEST)
    z = jnp.exp(logits - jnp.max(logits, axis=-1, keepdims=True))
    probs = z / jnp.sum(z, axis=-1, keepdims=True)
    lane = lax.broadcasted_iota(jnp.int32, probs.shape, 1)
    n = probs.shape[1]
    v1 = jnp.max(probs, axis=-1, keepdims=True)
    i1 = jnp.min(jnp.where(probs == v1, lane, n), axis=-1, keepdims=True)
    rest = jnp.where(lane == i1, -1.0, probs)
    v2 = jnp.max(rest, axis=-1, keepdims=True)
    i2 = jnp.min(jnp.where(rest == v2, lane, n), axis=-1, keepdims=True)
    total = v1 + v2
    idx_ref[...] = jnp.where(lane == 0, i1, jnp.where(lane == 1, i2, 0))
    wgt_ref[...] = jnp.where(lane == 0, v1 / total, jnp.where(lane == 1, v2 / total, 0.0))


def _router(x, mod, gain, router, seq):
    t, d = x.shape
    n = router.shape[1]
    tiles_per_seq = seq // ROW_TILE
    return pl.pallas_call(
        _router_kernel,
        out_shape=(jax.ShapeDtypeStruct((t, d), F32),
                   jax.ShapeDtypeStruct((t, n), jnp.int32),
                   jax.ShapeDtypeStruct((t, n), F32)),
        grid=(t // ROW_TILE,),
        in_specs=[
            pl.BlockSpec((ROW_TILE, d), lambda i: (i, 0)),
            pl.BlockSpec((None, 6, d), lambda i: (i // tiles_per_seq, 0, 0)),
            pl.BlockSpec((1, d), lambda i: (0, 0)),
            pl.BlockSpec((d, n), lambda i: (0, 0)),
        ],
        out_specs=(pl.BlockSpec((ROW_TILE, d), lambda i: (i, 0)),
                   pl.BlockSpec((ROW_TILE, n), lambda i: (i, 0)),
                   pl.BlockSpec((ROW_TILE, n), lambda i: (i, 0))),
        compiler_params=_params("parallel"),
        name="router",
    )(x, mod, gain, router)


def _routing_tables(expert_ids, n_experts, max_tiles):
    flat = expert_ids.reshape(-1)
    onehot = (flat[:, None] == jnp.arange(n_experts, dtype=jnp.int32)[None, :]).astype(jnp.int32)
    csum = jnp.cumsum(onehot, axis=0)
    rank = jnp.sum((csum - 1) * onehot, axis=1)
    counts = csum[-1]
    tiles = (counts + MOE_TILE - 1) // MOE_TILE
    tile_end = jnp.cumsum(tiles)
    tile_start = tile_end - tiles
    n_tiles = tile_end[-1]
    pos = tile_start[flat] * MOE_TILE + rank
    tile_ids = jnp.minimum(jnp.arange(max_tiles, dtype=jnp.int32), n_tiles - 1)
    tile_expert = jnp.sum((tile_ids[:, None] >= tile_end[None, :]).astype(jnp.int32), axis=1)
    token = jnp.arange(flat.shape[0], dtype=jnp.int32) // expert_ids.shape[1]
    src_token = jnp.zeros((max_tiles * MOE_TILE,), jnp.int32).at[pos].set(token)
    return pos, src_token, tile_expert, n_tiles.reshape(1)


def _row_copy(src_ref, row, dst_ref, slot, sem):
    return pltpu.make_async_copy(src_ref.at[pl.ds(row, 1)], dst_ref.at[pl.ds(slot, 1)], sem)


def _dispatch_kernel(nt_ref, idx_ref, src_ref, o_ref, buf_ref, sem):
    k = pl.program_id(0)
    rows = buf_ref.shape[0]

    @pl.when(k < nt_ref[0])
    def _():
        def issue(r, carry):
            _row_copy(src_ref, idx_ref[0, 0, r], buf_ref, r, sem).start()
            return carry

        lax.fori_loop(0, rows, issue, 0, unroll=8)
        pltpu.make_async_copy(src_ref.at[pl.ds(0, rows)], buf_ref, sem).wait()
        o_ref[...] = buf_ref[...].astype(o_ref.dtype)

    @pl.when(k >= nt_ref[0])
    def _():
        o_ref[...] = jnp.zeros_like(o_ref)


def _dispatch(h, src_token, n_tiles, max_tiles):
    d = h.shape[1]
    return pl.pallas_call(
        _dispatch_kernel,
        out_shape=jax.ShapeDtypeStruct((max_tiles * MOE_TILE, d), BF16),
        grid_spec=pltpu.PrefetchScalarGridSpec(
            num_scalar_prefetch=1,
            grid=(max_tiles,),
            in_specs=[
                pl.BlockSpec((1, 1, MOE_TILE), lambda k, nt: (k, 0, 0), memory_space=pltpu.SMEM),
                pl.BlockSpec(memory_space=pl.ANY),
            ],
            out_specs=pl.BlockSpec((MOE_TILE, d), lambda k, nt: (k, 0)),
            scratch_shapes=[pltpu.VMEM((MOE_TILE, d), F32), pltpu.SemaphoreType.DMA],
        ),
        compiler_params=_params("arbitrary"),
        name="moe_dispatch",
    )(n_tiles, src_token.reshape(max_tiles, 1, MOE_TILE), h)


def _moe_kernel(te_ref, nt_ref, x_ref, wg_ref, wu_ref, wd_ref, o_ref):
    k = pl.program_id(0)
    f = pl.program_id(1)

    @pl.when(f == 0)
    def _():
        o_ref[...] = jnp.zeros_like(o_ref)

    @pl.when(k < nt_ref[0])
    def _():
        o_ref[...] += _swiglu_tile(x_ref[...], wg_ref[...], wu_ref[...], wd_ref[...])


def _moe_ffn(xs, tile_expert, n_tiles, wg, wu, wd):
    p, d = xs.shape
    ff = wg.shape[2]
    n_f = ff // MOE_FF_TILE

    def col(k, f, nt):
        return jnp.where(k < nt[0], f, n_f - 1)

    return pl.pallas_call(
        _moe_kernel,
        out_shape=jax.ShapeDtypeStruct((p, d), F32),
        grid_spec=pltpu.PrefetchScalarGridSpec(
            num_scalar_prefetch=2,
            grid=(p // MOE_TILE, n_f),
            in_specs=[
                pl.BlockSpec((MOE_TILE, d), lambda k, f, te, nt: (k, 0)),
                pl.BlockSpec((None, d, MOE_FF_TILE), lambda k, f, te, nt: (te[k], 0, col(k, f, nt))),
                pl.BlockSpec((None, d, MOE_FF_TILE), lambda k, f, te, nt: (te[k], 0, col(k, f, nt))),
                pl.BlockSpec((None, MOE_FF_TILE, d), lambda k, f, te, nt: (te[k], col(k, f, nt), 0)),
            ],
            out_specs=pl.BlockSpec((MOE_TILE, d), lambda k, f, te, nt: (k, 0)),
        ),
        compiler_params=_params("arbitrary", "arbitrary"),
        name="moe_ffn",
    )(tile_expert, n_tiles, xs, wg, wu, wd)


def _combine_kernel(pos_ref, y_ref, x_ref, w_ref, mod_ref, o_ref, buf0_ref, buf1_ref, sem):
    rows = buf0_ref.shape[0]

    def issue(r, carry):
        _row_copy(y_ref, pos_ref[0, 0, 2 * r], buf0_ref, r, sem).start()
        _row_copy(y_ref, pos_ref[0, 0, 2 * r + 1], buf1_ref, r, sem).start()
        return carry

    lax.fori_loop(0, rows, issue, 0, unroll=4)
    pltpu.make_async_copy(y_ref.at[pl.ds(0, rows)], buf0_ref, sem).wait()
    pltpu.make_async_copy(y_ref.at[pl.ds(0, rows)], buf1_ref, sem).wait()
    w = w_ref[...]
    f = w[:, 0:1] * buf0_ref[...] + w[:, 1:2] * buf1_ref[...]
    o_ref[...] = x_ref[...] + mod_ref[5:6, :] * f


def _combine(y, pos, weights, x, mod, seq):
    t, d = x.shape
    n = weights.shape[1]
    tiles_per_seq = seq // COMBINE_TILE
    return pl.pallas_call(
        _combine_kernel,
        out_shape=jax.ShapeDtypeStruct((t, d), F32),
        grid=(t // COMBINE_TILE,),
        in_specs=[
            pl.BlockSpec((1, 1, 2 * COMBINE_TILE), lambda i: (i, 0, 0), memory_space=pltpu.SMEM),
            pl.BlockSpec(memory_space=pl.ANY),
            pl.BlockSpec((COMBINE_TILE, d), lambda i: (i, 0)),
            pl.BlockSpec((COMBINE_TILE, n), lambda i: (i, 0)),
            pl.BlockSpec((None, 6, d), lambda i: (i // tiles_per_seq, 0, 0)),
        ],
        out_specs=pl.BlockSpec((COMBINE_TILE, d), lambda i: (i, 0)),
        scratch_shapes=[pltpu.VMEM((COMBINE_TILE, d), F32), pltpu.VMEM((COMBINE_TILE, d), F32),
                        pltpu.SemaphoreType.DMA],
        compiler_params=_params("arbitrary"),
        name="moe_combine",
    )(pos.reshape(t // COMBINE_TILE, 1, 2 * COMBINE_TILE), y, x, weights, mod)


def _routed_ffn(x, mod, gain, router, wg, wu, wd, seq):
    t = x.shape[0]
    n_experts = router.shape[1]
    top_k = 2
    max_tiles = (t * top_k) // MOE_TILE + n_experts
    h, idx, wgt = _router(x, mod, gain, router, seq)
    pos, src_token, tile_expert, n_tiles = _routing_tables(idx[:, :top_k], n_experts, max_tiles)
    xs = _dispatch(h, src_token, n_tiles, max_tiles)
    y = _moe_ffn(xs, tile_expert, n_tiles, wg, wu, wd)
    return _combine(y, pos, wgt, x, mod, seq)


def kernel(x, c, norm1_g, w_ada, b_ada, w_in, q_norm_g, k_norm_g, conv_w, attn_out_g, conv_out_g,
           w_out, norm2_g, dense_w_gate, dense_w_up, dense_w_down, moe_router, moe_w_gate,
           moe_w_up, moe_w_down):
    batch, seq, d = x.shape
    depth = w_in.shape[0]
    assert seq % ROW_TILE == 0 and seq % Q_TILE == 0 and seq % COMBINE_TILE == 0
    tables = _rope_tables(seq)
    mod = _ada_mod(c, w_ada, b_ada)
    xt = x.reshape(batch * seq, d)
    for l in range(depth):
        p = _in_projection(xt, mod[l], norm1_g[l][None], w_in[l].astype(BF16), seq)
        attn = _attention(p, tables, q_norm_g[l][None], k_norm_g[l][None], attn_out_g[l][None],
                          batch, seq)
        conv = _short_conv(p, conv_w[l], conv_out_g[l][None], batch, seq)
        xt = _out_projection(attn, conv, w_out[l].astype(BF16), xt, mod[l], seq)
        i = l // 2
        if l % 2 == 0:
            xt = _dense_ffn(xt, mod[l], norm2_g[l][None], dense_w_gate[i].astype(BF16),
                            dense_w_up[i].astype(BF16), dense_w_down[i].astype(BF16), seq)
        else:
            xt = _routed_ffn(xt, mod[l], norm2_g[l][None], moe_router[i],
                             moe_w_gate[i].astype(BF16), moe_w_up[i].astype(BF16),
                             moe_w_down[i].astype(BF16), seq)
    return xt.reshape(batch, seq, d)
```

```python
import functools

import jax
import jax.numpy as jnp
from jax import lax
from jax.experimental import pallas as pl
from jax.experimental.pallas import tpu as pltpu

F32 = jnp.float32
BF16 = jnp.bfloat16

N_HEADS = 8
N_KV_HEADS = 2
HEAD_DIM = 128
GQA = N_HEADS // N_KV_HEADS
ATTN_W = N_HEADS * HEAD_DIM
KV_W = N_KV_HEADS * HEAD_DIM
ROPE_THETA = 10000.0
GRID_W = 64
CONV_GROUPS = 8
N_EXPERTS = 8
EPS = 1e-6
LOG2_E = 1.4426950408889634

V7X_VMEM_BYTES = 64 * 1024 * 1024
VMEM_LIMIT = V7X_VMEM_BYTES - 8 * 1024 * 1024

ROW_TILE = 1024
Q_TILE = 512
FF_TILE = 512
IN_TILE = 1536
OUT_TILE = 1024
ADA_TILE = 1024
MOE_TILE = 512
MOE_FF_TILE = 1024
COMBINE_TILE = 512


def _params(*sem):
    return pltpu.CompilerParams(dimension_semantics=sem, vmem_limit_bytes=VMEM_LIMIT)


def _sigmoid(x):
    return 1.0 / (1.0 + jnp.exp(-x))


def _modulated_norm(x, gain, shift, scale):
    ms = jnp.mean(x * x, axis=-1, keepdims=True)
    return (x * lax.rsqrt(ms + EPS) * gain) * (1.0 + scale) + shift


def _ada_kernel(c_ref, w_ref, b_ref, o_ref):
    c = c_ref[...]
    act = (c * _sigmoid(c)).astype(BF16)
    o_ref[...] = jnp.dot(act, w_ref[...].astype(BF16), preferred_element_type=F32) + b_ref[...]


def _ada_mod(c, w_ada, b_ada):
    depth, d, n = w_ada.shape
    b = c.shape[0]
    out = pl.pallas_call(
        _ada_kernel,
        out_shape=jax.ShapeDtypeStruct((depth, b, n), F32),
        grid=(depth, n // ADA_TILE),
        in_specs=[
            pl.BlockSpec((b, d), lambda l, j: (0, 0)),
            pl.BlockSpec((None, d, ADA_TILE), lambda l, j: (l, 0, j)),
            pl.BlockSpec((None, 1, ADA_TILE), lambda l, j: (l, 0, j)),
        ],
        out_specs=pl.BlockSpec((None, b, ADA_TILE), lambda l, j: (l, 0, j)),
        compiler_params=_params("parallel", "parallel"),
        name="ada_mod",
    )(c, w_ada, b_ada.reshape(depth, 1, n))
    return out.reshape(depth, b, 6, d)


def _inproj_kernel(x_ref, mod_ref, g_ref, w_ref, o_ref, h_ref):
    @pl.when(pl.program_id(1) == 0)
    def _():
        h = _modulated_norm(x_ref[...], g_ref[...], mod_ref[0:1, :], mod_ref[1:2, :])
        h_ref[...] = h.astype(BF16)

    o_ref[...] = jnp.dot(h_ref[...], w_ref[...], preferred_element_type=F32).astype(o_ref.dtype)


def _in_projection(x, mod, gain, w, seq):
    t, d = x.shape
    n = w.shape[1]
    tiles_per_seq = seq // ROW_TILE
    return pl.pallas_call(
        _inproj_kernel,
        out_shape=jax.ShapeDtypeStruct((t, n), BF16),
        grid=(t // ROW_TILE, n // IN_TILE),
        in_specs=[
            pl.BlockSpec((ROW_TILE, d), lambda i, j: (i, 0)),
            pl.BlockSpec((None, 6, d), lambda i, j: (i // tiles_per_seq, 0, 0)),
            pl.BlockSpec((1, d), lambda i, j: (0, 0)),
            pl.BlockSpec((d, IN_TILE), lambda i, j: (0, j)),
        ],
        out_specs=pl.BlockSpec((ROW_TILE, IN_TILE), lambda i, j: (i, j)),
        scratch_shapes=[pltpu.VMEM((ROW_TILE, d), BF16)],
        compiler_params=_params("parallel", "arbitrary"),
        name="in_projection",
    )(x, mod, gain, w)


def _rope_tables(seq):
    pos = jnp.arange(seq, dtype=jnp.int32)
    row = (pos // GRID_W).astype(F32)
    col = (pos % GRID_W).astype(F32)
    quarter = HEAD_DIM // 4
    inv = ROPE_THETA ** (-jnp.arange(quarter, dtype=F32) / quarter)
    ang_r = row[:, None] * inv
    ang_c = col[:, None] * inv
    ang = jnp.concatenate([ang_r, ang_r, ang_c, ang_c], axis=-1)
    cos, sin = jnp.cos(ang), jnp.sin(ang)
    first_half = (jnp.arange(HEAD_DIM) % (2 * quarter)) < quarter
    sin_up = jnp.where(first_half, -sin, 0.0)
    sin_dn = jnp.where(first_half, 0.0, sin)
    return cos, sin_up, sin_dn


def _norm_rope(x, gain, cos, sin_up, sin_dn):
    ms = jnp.mean(x * x, axis=-1, keepdims=True)
    y = x * lax.rsqrt(ms + EPS) * gain
    quarter = HEAD_DIM // 4
    return (y * cos + pltpu.roll(y, HEAD_DIM - quarter, 1) * sin_up
            + pltpu.roll(y, quarter, 1) * sin_dn)


def _attn_kernel(q_ref, k_ref, v_ref, cos_ref, su_ref, sd_ref, qg_ref, kg_ref, og_ref,
                 o_ref, kt_ref, v1_ref):
    qt = pl.program_id(2)

    @pl.when(qt == 0)
    def _():
        k = _norm_rope(k_ref[...].astype(F32), kg_ref[...], cos_ref[...], su_ref[...], sd_ref[...])
        kt_ref[...] = k.T.astype(BF16)
        v1_ref[:, :HEAD_DIM] = v_ref[...]
        v1_ref[:, HEAD_DIM:] = jnp.ones_like(v_ref)

    r0 = pl.multiple_of(qt * Q_TILE, Q_TILE)
    cos = cos_ref[pl.ds(r0, Q_TILE), :]
    s_up = su_ref[pl.ds(r0, Q_TILE), :]
    s_dn = sd_ref[pl.ds(r0, Q_TILE), :]
    scale = HEAD_DIM ** -0.5 * LOG2_E
    def scores(g):
        cols = slice(g * HEAD_DIM, (g + 1) * HEAD_DIM)
        q = _norm_rope(q_ref[:, cols].astype(F32), qg_ref[...], cos, s_up, s_dn) * scale
        return jnp.dot(q.astype(BF16), kt_ref[...], preferred_element_type=F32)

    s_next = scores(0)
    for g in range(GQA):
        cols = slice(g * HEAD_DIM, (g + 1) * HEAD_DIM)
        s = s_next
        if g + 1 < GQA:
            s_next = scores(g + 1)
        p = jnp.exp2((s - jnp.max(s, axis=-1, keepdims=True)).astype(BF16))
        ov = jnp.dot(p, v1_ref[...], preferred_element_type=F32)
        o = ov[:, :HEAD_DIM] / ov[:, HEAD_DIM:HEAD_DIM + 1]
        ms = jnp.mean(o * o, axis=-1, keepdims=True)
        o_ref[:, cols] = (o * lax.rsqrt(ms + EPS) * og_ref[:, cols]).astype(o_ref.dtype)


def _attention(p, tables, q_gain, k_gain, out_gain, batch, seq):
    t = p.shape[0]
    q_tiles = seq // Q_TILE
    group_w = GQA * HEAD_DIM
    k_blk0 = ATTN_W // HEAD_DIM
    v_blk0 = (ATTN_W + KV_W) // HEAD_DIM
    table_spec = pl.BlockSpec((seq, HEAD_DIM), lambda b, h, i: (0, 0))
    gain_spec = pl.BlockSpec((1, HEAD_DIM), lambda b, h, i: (0, 0))
    return pl.pallas_call(
        _attn_kernel,
        out_shape=jax.ShapeDtypeStruct((t, ATTN_W), BF16),
        grid=(batch, N_KV_HEADS, q_tiles),
        in_specs=[
            pl.BlockSpec((Q_TILE, group_w), lambda b, h, i: (b * q_tiles + i, h)),
            pl.BlockSpec((seq, HEAD_DIM), lambda b, h, i: (b, k_blk0 + h)),
            pl.BlockSpec((seq, HEAD_DIM), lambda b, h, i: (b, v_blk0 + h)),
            table_spec, table_spec, table_spec,
            gain_spec, gain_spec,
            pl.BlockSpec((1, group_w), lambda b, h, i: (0, h)),
        ],
        out_specs=pl.BlockSpec((Q_TILE, group_w), lambda b, h, i: (b * q_tiles + i, h)),
        scratch_shapes=[pltpu.VMEM((HEAD_DIM, seq), BF16), pltpu.VMEM((seq, 2 * HEAD_DIM), BF16)],
        compiler_params=_params("parallel", "parallel", "arbitrary"),
        name="attention",
    )(p, p, p, *tables, q_gain, k_gain, out_gain)


def _conv_kernel(h_ref, b_ref, c_ref, w_ref, g_ref, o_ref):
    u = c_ref[...].astype(F32) * h_ref[...].astype(F32)
    seq = u.shape[0]
    pos = lax.broadcasted_iota(jnp.int32, u.shape, 0)
    prev = jnp.where(pos == 0, 0.0, pltpu.roll(u, 1, 0))
    nxt = jnp.where(pos == seq - 1, 0.0, pltpu.roll(u, seq - 1, 0))
    y = prev * w_ref[0:1, :] + u * w_ref[1:2, :] + nxt * w_ref[2:3, :]
    y = b_ref[...].astype(F32) * y
    ms = jnp.mean(y * y, axis=-1, keepdims=True)
    o_ref[...] = (y * lax.rsqrt(ms + EPS) * g_ref[...]).astype(o_ref.dtype)


def _short_conv(p, conv_w, out_gain, batch, seq):
    t = p.shape[0]
    conv_w_total = conv_w.shape[1]
    gd = conv_w_total // CONV_GROUPS
    h0 = (ATTN_W + 2 * KV_W) // gd
    b0 = h0 + CONV_GROUPS
    c0 = b0 + CONV_GROUPS
    return pl.pallas_call(
        _conv_kernel,
        out_shape=jax.ShapeDtypeStruct((t, conv_w_total), BF16),
        grid=(batch, CONV_GROUPS),
        in_specs=[
            pl.BlockSpec((seq, gd), lambda b, g: (b, h0 + g)),
            pl.BlockSpec((seq, gd), lambda b, g: (b, b0 + g)),
            pl.BlockSpec((seq, gd), lambda b, g: (b, c0 + g)),
            pl.BlockSpec((3, gd), lambda b, g: (0, g)),
            pl.BlockSpec((1, gd), lambda b, g: (0, g)),
        ],
        out_specs=pl.BlockSpec((seq, gd), lambda b, g: (b, g)),
        compiler_params=_params("parallel", "parallel"),
        name="short_conv",
    )(p, p, p, conv_w, out_gain)


def _outproj_kernel(a_ref, c_ref, wa_ref, wc_ref, x_ref, mod_ref, o_ref):
    acc = jnp.dot(a_ref[...], wa_ref[...], preferred_element_type=F32)
    acc = acc + jnp.dot(c_ref[...], wc_ref[...], preferred_element_type=F32)
    o_ref[...] = x_ref[...] + mod_ref[2:3, :] * acc


def _out_projection(attn, conv, w, x, mod, seq):
    t, d = x.shape
    wa = attn.shape[1]
    wc = conv.shape[1]
    assert wa == wc
    tiles_per_seq = seq // ROW_TILE
    return pl.pallas_call(
        _outproj_kernel,
        out_shape=jax.ShapeDtypeStruct((t, d), F32),
        grid=(t // ROW_TILE, d // OUT_TILE),
        in_specs=[
            pl.BlockSpec((ROW_TILE, wa), lambda i, j: (i, 0)),
            pl.BlockSpec((ROW_TILE, wc), lambda i, j: (i, 0)),
            pl.BlockSpec((wa, OUT_TILE), lambda i, j: (0, j)),
            pl.BlockSpec((wc, OUT_TILE), lambda i, j: (1, j)),
            pl.BlockSpec((ROW_TILE, OUT_TILE), lambda i, j: (i, j)),
            pl.BlockSpec((None, 6, OUT_TILE), lambda i, j: (i // tiles_per_seq, 0, j)),
        ],
        out_specs=pl.BlockSpec((ROW_TILE, OUT_TILE), lambda i, j: (i, j)),
        compiler_params=_params("parallel", "parallel"),
        name="out_projection",
    )(attn, conv, w, w, x, mod)


def _swiglu_tile(h, wg, wu, wd, row_scale=None):
    g = jnp.dot(h, wg, preferred_element_type=F32)
    u = jnp.dot(h, wu, preferred_element_type=F32)
    a = g * _sigmoid(g) * u
    if row_scale is not None:
        a = a * row_scale
    return jnp.dot(a.astype(BF16), wd, preferred_element_type=F32)


def _ffn_kernel(x_ref, mod_ref, g_ref, wg_ref, wu_ref, wd_ref, o_ref, h_ref):
    f = pl.program_id(1)

    @pl.when(f == 0)
    def _():
        h = _modulated_norm(x_ref[...], g_ref[...], mod_ref[3:4, :], mod_ref[4:5, :])
        h_ref[...] = h.astype(BF16)
        o_ref[...] = jnp.zeros_like(o_ref)

    o_ref[...] += _swiglu_tile(h_ref[...], wg_ref[...], wu_ref[...], wd_ref[...])

    @pl.when(f == pl.num_programs(1) - 1)
    def _():
        o_ref[...] = x_ref[...] + mod_ref[5:6, :] * o_ref[...]


def _dense_ffn(x, mod, gain, wg, wu, wd, seq):
    t, d = x.shape
    ff = wg.shape[1]
    tiles_per_seq = seq // ROW_TILE
    return pl.pallas_call(
        _ffn_kernel,
        out_shape=jax.ShapeDtypeStruct((t, d), F32),
        grid=(t // ROW_TILE, ff // FF_TILE),
        in_specs=[
            pl.BlockSpec((ROW_TILE, d), lambda i, f: (i, 0), pipeline_mode=pl.Buffered(1)),
            pl.BlockSpec((None, 6, d), lambda i, f: (i // tiles_per_seq, 0, 0)),
            pl.BlockSpec((1, d), lambda i, f: (0, 0)),
            pl.BlockSpec((d, FF_TILE), lambda i, f: (0, f)),
            pl.BlockSpec((d, FF_TILE), lambda i, f: (0, f)),
            pl.BlockSpec((FF_TILE, d), lambda i, f: (f, 0)),
        ],
        out_specs=pl.BlockSpec((ROW_TILE, d), lambda i, f: (i, 0)),
        scratch_shapes=[pltpu.VMEM((ROW_TILE, d), BF16)],
        compiler_params=_params("parallel", "arbitrary"),
        name="dense_ffn",
    )(x, mod, gain, wg, wu, wd)


def _router_kernel(x_ref, mod_ref, g_ref, r_ref, h_ref, idx_ref, wgt_ref):
    h = _modulated_norm(x_ref[...], g_ref[...], mod_ref[3:4, :], mod_ref[4:5, :])
    h_ref[...] = h
    logits = jnp.dot(h, r_ref[...], preferred_element_type=F32, precision=lax.Precision.HIGHEST)
    z = jnp.exp(logits - jnp.max(logits, axis=-1, keepdims=True))
    probs = z / jnp.sum(z, axis=-1, keepdims=True)
    lane = lax.broadcasted_iota(jnp.int32, probs.shape, 1)
    n = probs.shape[1]
    v1 = jnp.max(probs, axis=-1, keepdims=True)
    i1 = jnp.min(jnp.where(probs == v1, lane, n), axis=-1, keepdims=True)
    rest = jnp.where(lane == i1, -1.0, probs)
    v2 = jnp.max(rest, axis=-1, keepdims=True)
    i2 = jnp.min(jnp.where(rest == v2, lane, n), axis=-1, keepdims=True)
    total = v1 + v2
    idx_ref[...] = jnp.where(lane == 0, i1, jnp.where(lane == 1, i2, 0))
    wgt_ref[...] = jnp.where(lane == 0, v1 / total, jnp.where(lane == 1, v2 / total, 0.0))


def _router(x, mod, gain, router, seq):
    t, d = x.shape
    n = router.shape[1]
    tiles_per_seq = seq // ROW_TILE
    return pl.pallas_call(
        _router_kernel,
        out_shape=(jax.ShapeDtypeStruct((t, d), F32),
                   jax.ShapeDtypeStruct((t, n), jnp.int32),
                   jax.ShapeDtypeStruct((t, n), F32)),
        grid=(t // ROW_TILE,),
        in_specs=[
            pl.BlockSpec((ROW_TILE, d), lambda i: (i, 0)),
            pl.BlockSpec((None, 6, d), lambda i: (i // tiles_per_seq, 0, 0)),
            pl.BlockSpec((1, d), lambda i: (0, 0)),
            pl.BlockSpec((d, n), lambda i: (0, 0)),
        ],
        out_specs=(pl.BlockSpec((ROW_TILE, d), lambda i: (i, 0)),
                   pl.BlockSpec((ROW_TILE, n), lambda i: (i, 0)),
                   pl.BlockSpec((ROW_TILE, n), lambda i: (i, 0))),
        compiler_params=_params("parallel"),
        name="router",
    )(x, mod, gain, router)


def _routing_tables(expert_ids, n_experts, max_tiles):
    flat = expert_ids.reshape(-1)
    onehot = (flat[:, None] == jnp.arange(n_experts, dtype=jnp.int32)[None, :]).astype(jnp.int32)
    csum = jnp.cumsum(onehot, axis=0)
    rank = jnp.sum((csum - 1) * onehot, axis=1)
    counts = csum[-1]
    tiles = (counts + MOE_TILE - 1) // MOE_TILE
    tile_end = jnp.cumsum(tiles)
    tile_start = tile_end - tiles
    n_tiles = tile_end[-1]
    pos = tile_start[flat] * MOE_TILE + rank
    tile_ids = jnp.minimum(jnp.arange(max_tiles, dtype=jnp.int32), n_tiles - 1)
    tile_expert = jnp.sum((tile_ids[:, None] >= tile_end[None, :]).astype(jnp.int32), axis=1)
    token = jnp.arange(flat.shape[0], dtype=jnp.int32) // expert_ids.shape[1]
    src_token = jnp.zeros((max_tiles * MOE_TILE,), jnp.int32).at[pos].set(token, unique_indices=True)
    return pos, src_token, tile_expert, n_tiles.reshape(1)


def _row_copy(src_ref, row, dst_ref, slot, sem):
    return pltpu.make_async_copy(src_ref.at[pl.ds(row, 1)], dst_ref.at[pl.ds(slot, 1)], sem)


def _moe_kernel(te_ref, nt_ref, idx0_ref, idxn_ref, h_ref, wg_ref, wu_ref, wd_ref, o_ref,
                xbuf_ref, hb_ref, sem, *, rows_per_step):
    k = pl.program_id(0)
    f = pl.program_id(1)
    n_tiles = nt_ref[0]
    slot = k % 2

    @pl.when((k == 0) & (f == 0))
    def _():
        def issue(r, carry):
            _row_copy(h_ref, idx0_ref[0, 0, r], xbuf_ref.at[0], r, sem.at[0]).start()
            return carry

        lax.fori_loop(0, MOE_TILE, issue, 0, unroll=8)

    @pl.when(f == 0)
    def _():
        o_ref[...] = jnp.zeros_like(o_ref)

        @pl.when(k <= n_tiles)
        def _():
            pltpu.make_async_copy(h_ref.at[pl.ds(0, MOE_TILE)], xbuf_ref.at[slot], sem.at[slot]).wait()
            hb_ref[...] = xbuf_ref[slot].astype(BF16)

    @pl.when(k < n_tiles)
    def _():
        o_ref[...] += _swiglu_tile(hb_ref[...], wg_ref[...], wu_ref[...], wd_ref[...])
        first = f * rows_per_step
        for r in range(rows_per_step):
            _row_copy(h_ref, idxn_ref[0, 0, first + r], xbuf_ref.at[1 - slot], first + r,
                      sem.at[1 - slot]).start()


def _moe_ffn(h, src_token, tile_expert, n_tiles, wg, wu, wd):
    d = h.shape[1]
    ff = wg.shape[2]
    n_f = ff // MOE_FF_TILE
    grid_tiles = tile_expert.shape[0]
    idx = src_token.reshape(grid_tiles, 1, MOE_TILE)

    def col(k, f, nt):
        return jnp.where(k < nt[0], f, n_f - 1)

    return pl.pallas_call(
        functools.partial(_moe_kernel, rows_per_step=MOE_TILE // n_f),
        out_shape=jax.ShapeDtypeStruct((grid_tiles * MOE_TILE, d), F32),
        grid_spec=pltpu.PrefetchScalarGridSpec(
            num_scalar_prefetch=2,
            grid=(grid_tiles, n_f),
            in_specs=[
                pl.BlockSpec((1, 1, MOE_TILE), lambda k, f, te, nt: (0, 0, 0), memory_space=pltpu.SMEM),
                pl.BlockSpec((1, 1, MOE_TILE),
                             lambda k, f, te, nt: (jnp.minimum(k + 1, grid_tiles - 1), 0, 0),
                             memory_space=pltpu.SMEM),
                pl.BlockSpec(memory_space=pl.ANY),
                pl.BlockSpec((None, d, MOE_FF_TILE), lambda k, f, te, nt: (te[k], 0, col(k, f, nt))),
                pl.BlockSpec((None, d, MOE_FF_TILE), lambda k, f, te, nt: (te[k], 0, col(k, f, nt))),
                pl.BlockSpec((None, MOE_FF_TILE, d), lambda k, f, te, nt: (te[k], col(k, f, nt), 0)),
            ],
            out_specs=pl.BlockSpec((MOE_TILE, d), lambda k, f, te, nt: (k, 0)),
            scratch_shapes=[pltpu.VMEM((2, MOE_TILE, d), F32), pltpu.VMEM((MOE_TILE, d), BF16),
                            pltpu.SemaphoreType.DMA((2,))],
        ),
        compiler_params=_params("arbitrary", "arbitrary"),
        name="moe_ffn",
    )(tile_expert, n_tiles, idx, idx, h, wg, wu, wd)


def _combine_kernel(pos_ref, y_ref, x_ref, w_ref, mod_ref, o_ref, buf0_ref, buf1_ref, sem):
    rows = buf0_ref.shape[0]

    def issue(r, carry):
        _row_copy(y_ref, pos_ref[0, 0, 2 * r], buf0_ref, r, sem).start()
        _row_copy(y_ref, pos_ref[0, 0, 2 * r + 1], buf1_ref, r, sem).start()
        return carry

    lax.fori_loop(0, rows, issue, 0, unroll=4)
    pltpu.make_async_copy(y_ref.at[pl.ds(0, rows)], buf0_ref, sem).wait()
    pltpu.make_async_copy(y_ref.at[pl.ds(0, rows)], buf1_ref, sem).wait()
    w = w_ref[...]
    f = w[:, 0:1] * buf0_ref[...] + w[:, 1:2] * buf1_ref[...]
    o_ref[...] = x_ref[...] + mod_ref[5:6, :] * f


def _combine(y, pos, weights, x, mod, seq):
    t, d = x.shape
    n = weights.shape[1]
    tiles_per_seq = seq // COMBINE_TILE
    return pl.pallas_call(
        _combine_kernel,
        out_shape=jax.ShapeDtypeStruct((t, d), F32),
        grid=(t // COMBINE_TILE,),
        in_specs=[
            pl.BlockSpec((1, 1, 2 * COMBINE_TILE), lambda i: (i, 0, 0), memory_space=pltpu.SMEM),
            pl.BlockSpec(memory_space=pl.ANY),
            pl.BlockSpec((COMBINE_TILE, d), lambda i: (i, 0)),
            pl.BlockSpec((COMBINE_TILE, n), lambda i: (i, 0)),
            pl.BlockSpec((None, 6, d), lambda i: (i // tiles_per_seq, 0, 0)),
        ],
        out_specs=pl.BlockSpec((COMBINE_TILE, d), lambda i: (i, 0)),
        scratch_shapes=[pltpu.VMEM((COMBINE_TILE, d), F32), pltpu.VMEM((COMBINE_TILE, d), F32),
                        pltpu.SemaphoreType.DMA],
        compiler_params=_params("arbitrary"),
        name="moe_combine",
    )(pos.reshape(t // COMBINE_TILE, 1, 2 * COMBINE_TILE), y, x, weights, mod)


def _routed_ffn(x, mod, gain, router, wg, wu, wd, seq):
    t = x.shape[0]
    n_experts = router.shape[1]
    top_k = 2
    grid_tiles = (t * top_k) // MOE_TILE + n_experts + 1
    h, idx, wgt = _router(x, mod, gain, router, seq)
    pos, src_token, tile_expert, n_tiles = _routing_tables(idx[:, :top_k], n_experts, grid_tiles)
    y = _moe_ffn(h, src_token, tile_expert, n_tiles, wg, wu, wd)
    return _combine(y, pos, wgt, x, mod, seq)


def kernel(x, c, norm1_g, w_ada, b_ada, w_in, q_norm_g, k_norm_g, conv_w, attn_out_g, conv_out_g,
           w_out, norm2_g, dense_w_gate, dense_w_up, dense_w_down, moe_router, moe_w_gate,
           moe_w_up, moe_w_down):
    batch, seq, d = x.shape
    depth = w_in.shape[0]
    assert seq % ROW_TILE == 0 and seq % Q_TILE == 0 and seq % COMBINE_TILE == 0
    tables = _rope_tables(seq)
    mod = _ada_mod(c, w_ada, b_ada)
    xt = x.reshape(batch * seq, d)
    for l in range(depth):
        p = _in_projection(xt, mod[l], norm1_g[l][None], w_in[l].astype(BF16), seq)
        attn = _attention(p, tables, q_norm_g[l][None], k_norm_g[l][None], attn_out_g[l][None],
                          batch, seq)
        conv = _short_conv(p, conv_w[l], conv_out_g[l][None], batch, seq)
        xt = _out_projection(attn, conv, w_out[l].astype(BF16), xt, mod[l], seq)
        i = l // 2
        if l % 2 == 0:
            xt = _dense_ffn(xt, mod[l], norm2_g[l][None], dense_w_gate[i].astype(BF16),
                            dense_w_up[i].astype(BF16), dense_w_down[i].astype(BF16), seq)
        else:
            xt = _routed_ffn(xt, mod[l], norm2_g[l][None], moe_router[i],
                             moe_w_gate[i].astype(BF16), moe_w_up[i].astype(BF16),
                             moe_w_down[i].astype(BF16), seq)
    return xt.reshape(batch, seq, d)
```

```python
import functools

import jax
import jax.numpy as jnp
from jax import lax
from jax.experimental import pallas as pl
from jax.experimental.pallas import tpu as pltpu

F32 = jnp.float32
BF16 = jnp.bfloat16

N_HEADS = 8
N_KV_HEADS = 2
HEAD_DIM = 128
GQA = N_HEADS // N_KV_HEADS
ATTN_W = N_HEADS * HEAD_DIM
KV_W = N_KV_HEADS * HEAD_DIM
ROPE_THETA = 10000.0
GRID_W = 64
CONV_GROUPS = 8
N_EXPERTS = 8
EPS = 1e-6

V7X_VMEM_BYTES = 64 * 1024 * 1024
VMEM_LIMIT = V7X_VMEM_BYTES - 8 * 1024 * 1024

ROW_TILE = 1024
Q_TILE = 512
FF_TILE = 512
IN_TILE = 1536
OUT_TILE = 1024
ADA_TILE = 1024
MOE_TILE = 512
MOE_FF_TILE = 1024
COMBINE_TILE = 512


def _params(*sem):
    return pltpu.CompilerParams(dimension_semantics=sem, vmem_limit_bytes=VMEM_LIMIT)


def _sigmoid(x):
    return 1.0 / (1.0 + jnp.exp(-x))


def _modulated_norm(x, gain, shift, scale):
    ms = jnp.mean(x * x, axis=-1, keepdims=True)
    return (x * lax.rsqrt(ms + EPS) * gain) * (1.0 + scale) + shift


def _ada_kernel(c_ref, w_ref, b_ref, o_ref):
    c = c_ref[...]
    act = (c * _sigmoid(c)).astype(BF16)
    o_ref[...] = jnp.dot(act, w_ref[...].astype(BF16), preferred_element_type=F32) + b_ref[...]


def _ada_mod(c, w_ada, b_ada):
    depth, d, n = w_ada.shape
    b = c.shape[0]
    out = pl.pallas_call(
        _ada_kernel,
        out_shape=jax.ShapeDtypeStruct((depth, b, n), F32),
        grid=(depth, n // ADA_TILE),
        in_specs=[
            pl.BlockSpec((b, d), lambda l, j: (0, 0)),
            pl.BlockSpec((None, d, ADA_TILE), lambda l, j: (l, 0, j)),
            pl.BlockSpec((None, 1, ADA_TILE), lambda l, j: (l, 0, j)),
        ],
        out_specs=pl.BlockSpec((None, b, ADA_TILE), lambda l, j: (l, 0, j)),
        compiler_params=_params("parallel", "parallel"),
        name="ada_mod",
    )(c, w_ada, b_ada.reshape(depth, 1, n))
    return out.reshape(depth, b, 6, d)


def _inproj_kernel(x_ref, mod_ref, g_ref, w_ref, o_ref, h_ref):
    @pl.when(pl.program_id(1) == 0)
    def _():
        h = _modulated_norm(x_ref[...], g_ref[...], mod_ref[0:1, :], mod_ref[1:2, :])
        h_ref[...] = h.astype(BF16)

    o_ref[...] = jnp.dot(h_ref[...], w_ref[...], preferred_element_type=F32).astype(o_ref.dtype)


def _in_projection(x, mod, gain, w, seq):
    t, d = x.shape
    n = w.shape[1]
    tiles_per_seq = seq // ROW_TILE
    return pl.pallas_call(
        _inproj_kernel,
        out_shape=jax.ShapeDtypeStruct((t, n), BF16),
        grid=(t // ROW_TILE, n // IN_TILE),
        in_specs=[
            pl.BlockSpec((ROW_TILE, d), lambda i, j: (i, 0)),
            pl.BlockSpec((None, 6, d), lambda i, j: (i // tiles_per_seq, 0, 0)),
            pl.BlockSpec((1, d), lambda i, j: (0, 0)),
            pl.BlockSpec((d, IN_TILE), lambda i, j: (0, j)),
        ],
        out_specs=pl.BlockSpec((ROW_TILE, IN_TILE), lambda i, j: (i, j)),
        scratch_shapes=[pltpu.VMEM((ROW_TILE, d), BF16)],
        compiler_params=_params("parallel", "arbitrary"),
        name="in_projection",
    )(x, mod, gain, w)


def _rope_tables(seq):
    pos = jnp.arange(seq, dtype=jnp.int32)
    row = (pos // GRID_W).astype(F32)
    col = (pos % GRID_W).astype(F32)
    quarter = HEAD_DIM // 4
    inv = ROPE_THETA ** (-jnp.arange(quarter, dtype=F32) / quarter)
    ang_r = row[:, None] * inv
    ang_c = col[:, None] * inv
    ang = jnp.concatenate([ang_r, ang_r, ang_c, ang_c], axis=-1)
    cos, sin = jnp.cos(ang), jnp.sin(ang)
    first_half = (jnp.arange(HEAD_DIM) % (2 * quarter)) < quarter
    sin_up = jnp.where(first_half, -sin, 0.0)
    sin_dn = jnp.where(first_half, 0.0, sin)
    return cos, sin_up, sin_dn


def _norm_rope(x, gain, cos, sin_up, sin_dn):
    ms = jnp.mean(x * x, axis=-1, keepdims=True)
    y = x * lax.rsqrt(ms + EPS) * gain
    quarter = HEAD_DIM // 4
    return (y * cos + pltpu.roll(y, HEAD_DIM - quarter, 1) * sin_up
            + pltpu.roll(y, quarter, 1) * sin_dn)


def _attn_kernel(q_ref, k_ref, v_ref, cos_ref, su_ref, sd_ref, qg_ref, kg_ref, og_ref,
                 o_ref, kt_ref):
    qt = pl.program_id(2)

    @pl.when(qt == 0)
    def _():
        k = _norm_rope(k_ref[...].astype(F32), kg_ref[...], cos_ref[...], su_ref[...], sd_ref[...])
        kt_ref[...] = k.T.astype(BF16)

    r0 = pl.multiple_of(qt * Q_TILE, Q_TILE)
    cos = cos_ref[pl.ds(r0, Q_TILE), :]
    s_up = su_ref[pl.ds(r0, Q_TILE), :]
    s_dn = sd_ref[pl.ds(r0, Q_TILE), :]
    scale = HEAD_DIM ** -0.5
    for g in range(GQA):
        cols = slice(g * HEAD_DIM, (g + 1) * HEAD_DIM)
        q = _norm_rope(q_ref[:, cols].astype(F32), qg_ref[...], cos, s_up, s_dn) * scale
        s = jnp.dot(q.astype(BF16), kt_ref[...], preferred_element_type=F32)
        p = jnp.exp(s - jnp.max(s, axis=-1, keepdims=True))
        denom = jnp.sum(p, axis=-1, keepdims=True)
        o = jnp.dot(p.astype(BF16), v_ref[...], preferred_element_type=F32) / denom
        ms = jnp.mean(o * o, axis=-1, keepdims=True)
        o_ref[:, cols] = (o * lax.rsqrt(ms + EPS) * og_ref[:, cols]).astype(o_ref.dtype)


def _attention(p, tables, q_gain, k_gain, out_gain, batch, seq):
    t = p.shape[0]
    q_tiles = seq // Q_TILE
    group_w = GQA * HEAD_DIM
    k_blk0 = ATTN_W // HEAD_DIM
    v_blk0 = (ATTN_W + KV_W) // HEAD_DIM
    table_spec = pl.BlockSpec((seq, HEAD_DIM), lambda b, h, i: (0, 0))
    gain_spec = pl.BlockSpec((1, HEAD_DIM), lambda b, h, i: (0, 0))
    return pl.pallas_call(
        _attn_kernel,
        out_shape=jax.ShapeDtypeStruct((t, ATTN_W), BF16),
        grid=(batch, N_KV_HEADS, q_tiles),
        in_specs=[
            pl.BlockSpec((Q_TILE, group_w), lambda b, h, i: (b * q_tiles + i, h)),
            pl.BlockSpec((seq, HEAD_DIM), lambda b, h, i: (b, k_blk0 + h)),
            pl.BlockSpec((seq, HEAD_DIM), lambda b, h, i: (b, v_blk0 + h)),
            table_spec, table_spec, table_spec,
            gain_spec, gain_spec,
            pl.BlockSpec((1, group_w), lambda b, h, i: (0, h)),
        ],
        out_specs=pl.BlockSpec((Q_TILE, group_w), lambda b, h, i: (b * q_tiles + i, h)),
        scratch_shapes=[pltpu.VMEM((HEAD_DIM, seq), BF16)],
        compiler_params=_params("parallel", "parallel", "arbitrary"),
        name="attention",
    )(p, p, p, *tables, q_gain, k_gain, out_gain)


def _conv_kernel(h_ref, b_ref, c_ref, w_ref, g_ref, o_ref):
    u = c_ref[...].astype(F32) * h_ref[...].astype(F32)
    seq = u.shape[0]
    pos = lax.broadcasted_iota(jnp.int32, u.shape, 0)
    prev = jnp.where(pos == 0, 0.0, pltpu.roll(u, 1, 0))
    nxt = jnp.where(pos == seq - 1, 0.0, pltpu.roll(u, seq - 1, 0))
    y = prev * w_ref[0:1, :] + u * w_ref[1:2, :] + nxt * w_ref[2:3, :]
    y = b_ref[...].astype(F32) * y
    ms = jnp.mean(y * y, axis=-1, keepdims=True)
    o_ref[...] = (y * lax.rsqrt(ms + EPS) * g_ref[...]).astype(o_ref.dtype)


def _short_conv(p, conv_w, out_gain, batch, seq):
    t = p.shape[0]
    conv_w_total = conv_w.shape[1]
    gd = conv_w_total // CONV_GROUPS
    h0 = (ATTN_W + 2 * KV_W) // gd
    b0 = h0 + CONV_GROUPS
    c0 = b0 + CONV_GROUPS
    return pl.pallas_call(
        _conv_kernel,
        out_shape=jax.ShapeDtypeStruct((t, conv_w_total), BF16),
        grid=(batch, CONV_GROUPS),
        in_specs=[
            pl.BlockSpec((seq, gd), lambda b, g: (b, h0 + g)),
            pl.BlockSpec((seq, gd), lambda b, g: (b, b0 + g)),
            pl.BlockSpec((seq, gd), lambda b, g: (b, c0 + g)),
            pl.BlockSpec((3, gd), lambda b, g: (0, g)),
            pl.BlockSpec((1, gd), lambda b, g: (0, g)),
        ],
        out_specs=pl.BlockSpec((seq, gd), lambda b, g: (b, g)),
        compiler_params=_params("parallel", "parallel"),
        name="short_conv",
    )(p, p, p, conv_w, out_gain)


def _outproj_kernel(a_ref, c_ref, wa_ref, wc_ref, x_ref, mod_ref, o_ref):
    acc = jnp.dot(a_ref[...], wa_ref[...], preferred_element_type=F32)
    acc = acc + jnp.dot(c_ref[...], wc_ref[...], preferred_element_type=F32)
    o_ref[...] = x_ref[...] + mod_ref[2:3, :] * acc


def _out_projection(attn, conv, w, x, mod, seq):
    t, d = x.shape
    wa = attn.shape[1]
    wc = conv.shape[1]
    assert wa == wc
    tiles_per_seq = seq // ROW_TILE
    return pl.pallas_call(
        _outproj_kernel,
        out_shape=jax.ShapeDtypeStruct((t, d), F32),
        grid=(t // ROW_TILE, d // OUT_TILE),
        in_specs=[
            pl.BlockSpec((ROW_TILE, wa), lambda i, j: (i, 0)),
            pl.BlockSpec((ROW_TILE, wc), lambda i, j: (i, 0)),
            pl.BlockSpec((wa, OUT_TILE), lambda i, j: (0, j)),
            pl.BlockSpec((wc, OUT_TILE), lambda i, j: (1, j)),
            pl.BlockSpec((ROW_TILE, OUT_TILE), lambda i, j: (i, j)),
            pl.BlockSpec((None, 6, OUT_TILE), lambda i, j: (i // tiles_per_seq, 0, j)),
        ],
        out_specs=pl.BlockSpec((ROW_TILE, OUT_TILE), lambda i, j: (i, j)),
        compiler_params=_params("parallel", "parallel"),
        name="out_projection",
    )(attn, conv, w, w, x, mod)


def _swiglu_tile(h, wg, wu, wd, row_scale=None):
    g = jnp.dot(h, wg, preferred_element_type=F32)
    u = jnp.dot(h, wu, preferred_element_type=F32)
    a = g * _sigmoid(g) * u
    if row_scale is not None:
        a = a * row_scale
    return jnp.dot(a.astype(BF16), wd, preferred_element_type=F32)


def _ffn_kernel(x_ref, mod_ref, g_ref, wg_ref, wu_ref, wd_ref, o_ref, h_ref):
    f = pl.program_id(1)

    @pl.when(f == 0)
    def _():
        h = _modulated_norm(x_ref[...], g_ref[...], mod_ref[3:4, :], mod_ref[4:5, :])
        h_ref[...] = h.astype(BF16)
        o_ref[...] = jnp.zeros_like(o_ref)

    o_ref[...] += _swiglu_tile(h_ref[...], wg_ref[...], wu_ref[...], wd_ref[...])

    @pl.when(f == pl.num_programs(1) - 1)
    def _():
        o_ref[...] = x_ref[...] + mod_ref[5:6, :] * o_ref[...]


def _dense_ffn(x, mod, gain, wg, wu, wd, seq):
    t, d = x.shape
    ff = wg.shape[1]
    tiles_per_seq = seq // ROW_TILE
    return pl.pallas_call(
        _ffn_kernel,
        out_shape=jax.ShapeDtypeStruct((t, d), F32),
        grid=(t // ROW_TILE, ff // FF_TILE),
        in_specs=[
            pl.BlockSpec((ROW_TILE, d), lambda i, f: (i, 0), pipeline_mode=pl.Buffered(1)),
            pl.BlockSpec((None, 6, d), lambda i, f: (i // tiles_per_seq, 0, 0)),
            pl.BlockSpec((1, d), lambda i, f: (0, 0)),
            pl.BlockSpec((d, FF_TILE), lambda i, f: (0, f)),
            pl.BlockSpec((d, FF_TILE), lambda i, f: (0, f)),
            pl.BlockSpec((FF_TILE, d), lambda i, f: (f, 0)),
        ],
        out_specs=pl.BlockSpec((ROW_TILE, d), lambda i, f: (i, 0)),
        scratch_shapes=[pltpu.VMEM((ROW_TILE, d), BF16)],
        compiler_params=_params("parallel", "arbitrary"),
        name="dense_ffn",
    )(x, mod, gain, wg, wu, wd)


def _router_kernel(x_ref, mod_ref, g_ref, r_ref, h_ref, idx_ref, wgt_ref):
    h = _modulated_norm(x_ref[...], g_ref[...], mod_ref[3:4, :], mod_ref[4:5, :])
    h_ref[...] = h
    logits = jnp.dot(h, r_ref[...], preferred_element_type=F32, precision=lax.Precision.HIGHEST)
    z = jnp.exp(logits - jnp.max(logits, axis=-1, keepdims=True))
    probs = z / jnp.sum(z, axis=-1, keepdims=True)
    lane = lax.broadcasted_iota(jnp.int32, probs.shape, 1)
    n = probs.shape[1]
    v1 = jnp.max(probs, axis=-1, keepdims=True)
    i1 = jnp.min(jnp.where(probs == v1, lane, n), axis=-1, keepdims=True)
    rest = jnp.where(lane == i1, -1.0, probs)
    v2 = jnp.max(rest, axis=-1, keepdims=True)
    i2 = jnp.min(jnp.where(rest == v2, lane, n), axis=-1, keepdims=True)
    total = v1 + v2
    idx_ref[...] = jnp.where(lane == 0, i1, jnp.where(lane == 1, i2, 0))
    wgt_ref[...] = jnp.where(lane == 0, v1 / total, jnp.where(lane == 1, v2 / total, 0.0))


def _router(x, mod, gain, router, seq):
    t, d = x.shape
    n = router.shape[1]
    tiles_per_seq = seq // ROW_TILE
    return pl.pallas_call(
        _router_kernel,
        out_shape=(jax.ShapeDtypeStruct((t, d), F32),
                   jax.ShapeDtypeStruct((t, n), jnp.int32),
                   jax.ShapeDtypeStruct((t, n), F32)),
        grid=(t // ROW_TILE,),
        in_specs=[
            pl.BlockSpec((ROW_TILE, d), lambda i: (i, 0)),
            pl.BlockSpec((None, 6, d), lambda i: (i // tiles_per_seq, 0, 0)),
            pl.BlockSpec((1, d), lambda i: (0, 0)),
            pl.BlockSpec((d, n), lambda i: (0, 0)),
        ],
        out_specs=(pl.BlockSpec((ROW_TILE, d), lambda i: (i, 0)),
                   pl.BlockSpec((ROW_TILE, n), lambda i: (i, 0)),
                   pl.BlockSpec((ROW_TILE, n), lambda i: (i, 0))),
        compiler_params=_params("parallel"),
        name="router",
    )(x, mod, gain, router)


def _routing_tables(expert_ids, n_experts, max_tiles):
    flat = expert_ids.reshape(-1)
    onehot = (flat[:, None] == jnp.arange(n_experts, dtype=jnp.int32)[None, :]).astype(jnp.int32)
    csum = jnp.cumsum(onehot, axis=0)
    rank = jnp.sum((csum - 1) * onehot, axis=1)
    counts = csum[-1]
    tiles = (counts + MOE_TILE - 1) // MOE_TILE
    tile_end = jnp.cumsum(tiles)
    tile_start = tile_end - tiles
    n_tiles = tile_end[-1]
    pos = tile_start[flat] * MOE_TILE + rank
    tile_ids = jnp.minimum(jnp.arange(max_tiles, dtype=jnp.int32), n_tiles - 1)
    tile_expert = jnp.sum((tile_ids[:, None] >= tile_end[None, :]).astype(jnp.int32), axis=1)
    token = jnp.arange(flat.shape[0], dtype=jnp.int32) // expert_ids.shape[1]
    src_token = jnp.zeros((max_tiles * MOE_TILE,), jnp.int32).at[pos].set(token, unique_indices=True)
    return pos, src_token, tile_expert, n_tiles.reshape(1)


def _row_copy(src_ref, row, dst_ref, slot, sem):
    return pltpu.make_async_copy(src_ref.at[pl.ds(row, 1)], dst_ref.at[pl.ds(slot, 1)], sem)


def _moe_kernel(te_ref, nt_ref, idx0_ref, idxn_ref, h_ref, wg_ref, wu_ref, wd_ref, o_ref,
                xbuf_ref, hb_ref, sem, *, rows_per_step):
    k = pl.program_id(0)
    f = pl.program_id(1)
    n_tiles = nt_ref[0]
    slot = k % 2

    @pl.when((k == 0) & (f == 0))
    def _():
        def issue(r, carry):
            _row_copy(h_ref, idx0_ref[0, 0, r], xbuf_ref.at[0], r, sem.at[0]).start()
            return carry

        lax.fori_loop(0, MOE_TILE, issue, 0, unroll=8)

    @pl.when(f == 0)
    def _():
        o_ref[...] = jnp.zeros_like(o_ref)

        @pl.when(k <= n_tiles)
        def _():
            pltpu.make_async_copy(h_ref.at[pl.ds(0, MOE_TILE)], xbuf_ref.at[slot], sem.at[slot]).wait()
            hb_ref[...] = xbuf_ref[slot].astype(BF16)

    @pl.when(k < n_tiles)
    def _():
        o_ref[...] += _swiglu_tile(hb_ref[...], wg_ref[...], wu_ref[...], wd_ref[...])

        @pl.when(f * rows_per_step < MOE_TILE)
        def _():
            first = f * rows_per_step
            for r in range(rows_per_step):
                _row_copy(h_ref, idxn_ref[0, 0, first + r], xbuf_ref.at[1 - slot], first + r,
                          sem.at[1 - slot]).start()


def _moe_ffn(h, src_token, tile_expert, n_tiles, wg, wu, wd):
    d = h.shape[1]
    ff = wg.shape[2]
    n_f = ff // MOE_FF_TILE
    grid_tiles = tile_expert.shape[0]
    idx = src_token.reshape(grid_tiles, 1, MOE_TILE)

    def col(k, f, nt):
        return jnp.where(k < nt[0], f, n_f - 1)

    return pl.pallas_call(
        functools.partial(_moe_kernel, rows_per_step=MOE_TILE // max(1, n_f // 2)),
        out_shape=jax.ShapeDtypeStruct((grid_tiles * MOE_TILE, d), F32),
        grid_spec=pltpu.PrefetchScalarGridSpec(
            num_scalar_prefetch=2,
            grid=(grid_tiles, n_f),
            in_specs=[
                pl.BlockSpec((1, 1, MOE_TILE), lambda k, f, te, nt: (0, 0, 0), memory_space=pltpu.SMEM),
                pl.BlockSpec((1, 1, MOE_TILE),
                             lambda k, f, te, nt: (jnp.minimum(k + 1, grid_tiles - 1), 0, 0),
                             memory_space=pltpu.SMEM),
                pl.BlockSpec(memory_space=pl.ANY),
                pl.BlockSpec((None, d, MOE_FF_TILE), lambda k, f, te, nt: (te[k], 0, col(k, f, nt))),
                pl.BlockSpec((None, d, MOE_FF_TILE), lambda k, f, te, nt: (te[k], 0, col(k, f, nt))),
                pl.BlockSpec((None, MOE_FF_TILE, d), lambda k, f, te, nt: (te[k], col(k, f, nt), 0)),
            ],
            out_specs=pl.BlockSpec((MOE_TILE, d), lambda k, f, te, nt: (k, 0)),
            scratch_shapes=[pltpu.VMEM((2, MOE_TILE, d), F32), pltpu.VMEM((MOE_TILE, d), BF16),
                            pltpu.SemaphoreType.DMA((2,))],
        ),
        compiler_params=_params("arbitrary", "arbitrary"),
        name="moe_ffn",
    )(tile_expert, n_tiles, idx, idx, h, wg, wu, wd)


def _combine_kernel(pos_ref, y_ref, x_ref, w_ref, mod_ref, o_ref, buf0_ref, buf1_ref, sem):
    rows = buf0_ref.shape[0]

    def issue(r, carry):
        _row_copy(y_ref, pos_ref[0, 0, 2 * r], buf0_ref, r, sem).start()
        _row_copy(y_ref, pos_ref[0, 0, 2 * r + 1], buf1_ref, r, sem).start()
        return carry

    lax.fori_loop(0, rows, issue, 0, unroll=4)
    pltpu.make_async_copy(y_ref.at[pl.ds(0, rows)], buf0_ref, sem).wait()
    pltpu.make_async_copy(y_ref.at[pl.ds(0, rows)], buf1_ref, sem).wait()
    w = w_ref[...]
    f = w[:, 0:1] * buf0_ref[...] + w[:, 1:2] * buf1_ref[...]
    o_ref[...] = x_ref[...] + mod_ref[5:6, :] * f


def _combine(y, pos, weights, x, mod, seq):
    t, d = x.shape
    n = weights.shape[1]
    tiles_per_seq = seq // COMBINE_TILE
    return pl.pallas_call(
        _combine_kernel,
        out_shape=jax.ShapeDtypeStruct((t, d), F32),
        grid=(t // COMBINE_TILE,),
        in_specs=[
            pl.BlockSpec((1, 1, 2 * COMBINE_TILE), lambda i: (i, 0, 0), memory_space=pltpu.SMEM),
            pl.BlockSpec(memory_space=pl.ANY),
            pl.BlockSpec((COMBINE_TILE, d), lambda i: (i, 0)),
            pl.BlockSpec((COMBINE_TILE, n), lambda i: (i, 0)),
            pl.BlockSpec((None, 6, d), lambda i: (i // tiles_per_seq, 0, 0)),
        ],
        out_specs=pl.BlockSpec((COMBINE_TILE, d), lambda i: (i, 0)),
        scratch_shapes=[pltpu.VMEM((COMBINE_TILE, d), F32), pltpu.VMEM((COMBINE_TILE, d), F32),
                        pltpu.SemaphoreType.DMA],
        compiler_params=_params("arbitrary"),
        name="moe_combine",
    )(pos.reshape(t // COMBINE_TILE, 1, 2 * COMBINE_TILE), y, x, weights, mod)


def _routed_ffn(x, mod, gain, router, wg, wu, wd, seq):
    t = x.shape[0]
    n_experts = router.shape[1]
    top_k = 2
    grid_tiles = (t * top_k) // MOE_TILE + n_experts + 1
    h, idx, wgt = _router(x, mod, gain, router, seq)
    pos, src_token, tile_expert, n_tiles = _routing_tables(idx[:, :top_k], n_experts, grid_tiles)
    y = _moe_ffn(h, src_token, tile_expert, n_tiles, wg, wu, wd)
    return _combine(y, pos, wgt, x, mod, seq)


def kernel(x, c, norm1_g, w_ada, b_ada, w_in, q_norm_g, k_norm_g, conv_w, attn_out_g, conv_out_g,
           w_out, norm2_g, dense_w_gate, dense_w_up, dense_w_down, moe_router, moe_w_gate,
           moe_w_up, moe_w_down):
    batch, seq, d = x.shape
    depth = w_in.shape[0]
    assert seq % ROW_TILE == 0 and seq % Q_TILE == 0 and seq % COMBINE_TILE == 0
    tables = _rope_tables(seq)
    mod = _ada_mod(c, w_ada, b_ada)
    xt = x.reshape(batch * seq, d)
    for l in range(depth):
        p = _in_projection(xt, mod[l], norm1_g[l][None], w_in[l].astype(BF16), seq)
        attn = _attention(p, tables, q_norm_g[l][None], k_norm_g[l][None], attn_out_g[l][None],
                          batch, seq)
        conv = _short_conv(p, conv_w[l], conv_out_g[l][None], batch, seq)
        xt = _out_projection(attn, conv, w_out[l].astype(BF16), xt, mod[l], seq)
        i = l // 2
        if l % 2 == 0:
            xt = _dense_ffn(xt, mod[l], norm2_g[l][None], dense_w_gate[i].astype(BF16),
                            dense_w_up[i].astype(BF16), dense_w_down[i].astype(BF16), seq)
        else:
            xt = _routed_ffn(xt, mod[l], norm2_g[l][None], moe_router[i],
                             moe_w_gate[i].astype(BF16), moe_w_up[i].astype(BF16),
                             moe_w_down[i].astype(BF16), seq)
    return xt.reshape(batch, seq, d)
```

```python
import functools

import jax
import jax.numpy as jnp
from jax import lax
from jax.experimental import pallas as pl
from jax.experimental.pallas import tpu as pltpu

F32 = jnp.float32
BF16 = jnp.bfloat16

N_HEADS = 8
N_KV_HEADS = 2
HEAD_DIM = 128
GQA = N_HEADS // N_KV_HEADS
ATTN_W = N_HEADS * HEAD_DIM
KV_W = N_KV_HEADS * HEAD_DIM
ROPE_THETA = 10000.0
GRID_W = 64
CONV_GROUPS = 8
N_EXPERTS = 8
EPS = 1e-6

V7X_VMEM_BYTES = 64 * 1024 * 1024
VMEM_LIMIT = V7X_VMEM_BYTES - 8 * 1024 * 1024

ROW_TILE = 1024
Q_TILE = 512
FF_TILE = 512
IN_TILE = 1536
OUT_TILE = 1024
ADA_TILE = 1024
MOE_TILE = 512
MOE_FF_TILE = 1024
COMBINE_TILE = 512


def _params(*sem):
    return pltpu.CompilerParams(dimension_semantics=sem, vmem_limit_bytes=VMEM_LIMIT)


def _sigmoid(x):
    return 1.0 / (1.0 + jnp.exp(-x))


def _modulated_norm(x, gain, shift, scale):
    ms = jnp.mean(x * x, axis=-1, keepdims=True)
    return (x * lax.rsqrt(ms + EPS) * gain) * (1.0 + scale) + shift


def _ada_kernel(c_ref, w_ref, b_ref, o_ref):
    c = c_ref[...]
    act = (c * _sigmoid(c)).astype(BF16)
    o_ref[...] = jnp.dot(act, w_ref[...].astype(BF16), preferred_element_type=F32) + b_ref[...]


def _ada_mod(c, w_ada, b_ada):
    depth, d, n = w_ada.shape
    b = c.shape[0]
    out = pl.pallas_call(
        _ada_kernel,
        out_shape=jax.ShapeDtypeStruct((depth, b, n), F32),
        grid=(depth, n // ADA_TILE),
        in_specs=[
            pl.BlockSpec((b, d), lambda l, j: (0, 0)),
            pl.BlockSpec((None, d, ADA_TILE), lambda l, j: (l, 0, j)),
            pl.BlockSpec((None, 1, ADA_TILE), lambda l, j: (l, 0, j)),
        ],
        out_specs=pl.BlockSpec((None, b, ADA_TILE), lambda l, j: (l, 0, j)),
        compiler_params=_params("parallel", "parallel"),
        name="ada_mod",
    )(c, w_ada, b_ada.reshape(depth, 1, n))
    return out.reshape(depth, b, 6, d)


def _inproj_kernel(x_ref, mod_ref, g_ref, w_ref, o_ref, h_ref):
    @pl.when(pl.program_id(1) == 0)
    def _():
        h = _modulated_norm(x_ref[...], g_ref[...], mod_ref[0:1, :], mod_ref[1:2, :])
        h_ref[...] = h.astype(BF16)

    o_ref[...] = jnp.dot(h_ref[...], w_ref[...], preferred_element_type=F32).astype(o_ref.dtype)


def _in_projection(x, mod, gain, w, seq):
    t, d = x.shape
    n = w.shape[1]
    tiles_per_seq = seq // ROW_TILE
    return pl.pallas_call(
        _inproj_kernel,
        out_shape=jax.ShapeDtypeStruct((t, n), BF16),
        grid=(t // ROW_TILE, n // IN_TILE),
        in_specs=[
            pl.BlockSpec((ROW_TILE, d), lambda i, j: (i, 0)),
            pl.BlockSpec((None, 6, d), lambda i, j: (i // tiles_per_seq, 0, 0)),
            pl.BlockSpec((1, d), lambda i, j: (0, 0)),
            pl.BlockSpec((d, IN_TILE), lambda i, j: (0, j)),
        ],
        out_specs=pl.BlockSpec((ROW_TILE, IN_TILE), lambda i, j: (i, j)),
        scratch_shapes=[pltpu.VMEM((ROW_TILE, d), BF16)],
        compiler_params=_params("parallel", "arbitrary"),
        name="in_projection",
    )(x, mod, gain, w)


def _rope_tables(seq):
    pos = jnp.arange(seq, dtype=jnp.int32)
    row = (pos // GRID_W).astype(F32)
    col = (pos % GRID_W).astype(F32)
    quarter = HEAD_DIM // 4
    inv = ROPE_THETA ** (-jnp.arange(quarter, dtype=F32) / quarter)
    ang_r = row[:, None] * inv
    ang_c = col[:, None] * inv
    ang = jnp.concatenate([ang_r, ang_r, ang_c, ang_c], axis=-1)
    cos, sin = jnp.cos(ang), jnp.sin(ang)
    first_half = (jnp.arange(HEAD_DIM) % (2 * quarter)) < quarter
    sin_up = jnp.where(first_half, -sin, 0.0)
    sin_dn = jnp.where(first_half, 0.0, sin)
    return cos, sin_up, sin_dn


def _norm_rope(x, gain, cos, sin_up, sin_dn):
    ms = jnp.mean(x * x, axis=-1, keepdims=True)
    y = x * lax.rsqrt(ms + EPS) * gain
    quarter = HEAD_DIM // 4
    return (y * cos + pltpu.roll(y, HEAD_DIM - quarter, 1) * sin_up
            + pltpu.roll(y, quarter, 1) * sin_dn)


def _attn_kernel(q_ref, k_ref, v_ref, cos_ref, su_ref, sd_ref, qg_ref, kg_ref, og_ref,
                 o_ref, kt_ref):
    qt = pl.program_id(2)

    @pl.when(qt == 0)
    def _():
        k = _norm_rope(k_ref[...].astype(F32), kg_ref[...], cos_ref[...], su_ref[...], sd_ref[...])
        kt_ref[...] = k.T.astype(BF16)

    r0 = pl.multiple_of(qt * Q_TILE, Q_TILE)
    cos = cos_ref[pl.ds(r0, Q_TILE), :]
    s_up = su_ref[pl.ds(r0, Q_TILE), :]
    s_dn = sd_ref[pl.ds(r0, Q_TILE), :]
    scale = HEAD_DIM ** -0.5
    for g in range(GQA):
        cols = slice(g * HEAD_DIM, (g + 1) * HEAD_DIM)
        q = _norm_rope(q_ref[:, cols].astype(F32), qg_ref[...], cos, s_up, s_dn) * scale
        s = jnp.dot(q.astype(BF16), kt_ref[...], preferred_element_type=F32)
        p = jnp.exp(s - jnp.max(s, axis=-1, keepdims=True))
        denom = jnp.sum(p, axis=-1, keepdims=True)
        o = jnp.dot(p.astype(BF16), v_ref[...], preferred_element_type=F32) / denom
        ms = jnp.mean(o * o, axis=-1, keepdims=True)
        o_ref[:, cols] = (o * lax.rsqrt(ms + EPS) * og_ref[:, cols]).astype(o_ref.dtype)


def _attention(p, tables, q_gain, k_gain, out_gain, batch, seq):
    t = p.shape[0]
    q_tiles = seq // Q_TILE
    group_w = GQA * HEAD_DIM
    k_blk0 = ATTN_W // HEAD_DIM
    v_blk0 = (ATTN_W + KV_W) // HEAD_DIM
    table_spec = pl.BlockSpec((seq, HEAD_DIM), lambda b, h, i: (0, 0))
    gain_spec = pl.BlockSpec((1, HEAD_DIM), lambda b, h, i: (0, 0))
    return pl.pallas_call(
        _attn_kernel,
        out_shape=jax.ShapeDtypeStruct((t, ATTN_W), BF16),
        grid=(batch, N_KV_HEADS, q_tiles),
        in_specs=[
            pl.BlockSpec((Q_TILE, group_w), lambda b, h, i: (b * q_tiles + i, h)),
            pl.BlockSpec((seq, HEAD_DIM), lambda b, h, i: (b, k_blk0 + h)),
            pl.BlockSpec((seq, HEAD_DIM), lambda b, h, i: (b, v_blk0 + h)),
            table_spec, table_spec, table_spec,
            gain_spec, gain_spec,
            pl.BlockSpec((1, group_w), lambda b, h, i: (0, h)),
        ],
        out_specs=pl.BlockSpec((Q_TILE, group_w), lambda b, h, i: (b * q_tiles + i, h)),
        scratch_shapes=[pltpu.VMEM((HEAD_DIM, seq), BF16)],
        compiler_params=_params("parallel", "parallel", "arbitrary"),
        name="attention",
    )(p, p, p, *tables, q_gain, k_gain, out_gain)


def _conv_kernel(h_ref, b_ref, c_ref, w_ref, g_ref, o_ref):
    u = c_ref[...].astype(F32) * h_ref[...].astype(F32)
    seq = u.shape[0]
    pos = lax.broadcasted_iota(jnp.int32, u.shape, 0)
    prev = jnp.where(pos == 0, 0.0, pltpu.roll(u, 1, 0))
    nxt = jnp.where(pos == seq - 1, 0.0, pltpu.roll(u, seq - 1, 0))
    y = prev * w_ref[0:1, :] + u * w_ref[1:2, :] + nxt * w_ref[2:3, :]
    y = b_ref[...].astype(F32) * y
    ms = jnp.mean(y * y, axis=-1, keepdims=True)
    o_ref[...] = (y * lax.rsqrt(ms + EPS) * g_ref[...]).astype(o_ref.dtype)


def _short_conv(p, conv_w, out_gain, batch, seq):
    t = p.shape[0]
    conv_w_total = conv_w.shape[1]
    gd = conv_w_total // CONV_GROUPS
    h0 = (ATTN_W + 2 * KV_W) // gd
    b0 = h0 + CONV_GROUPS
    c0 = b0 + CONV_GROUPS
    return pl.pallas_call(
        _conv_kernel,
        out_shape=jax.ShapeDtypeStruct((t, conv_w_total), BF16),
        grid=(batch, CONV_GROUPS),
        in_specs=[
            pl.BlockSpec((seq, gd), lambda b, g: (b, h0 + g)),
            pl.BlockSpec((seq, gd), lambda b, g: (b, b0 + g)),
            pl.BlockSpec((seq, gd), lambda b, g: (b, c0 + g)),
            pl.BlockSpec((3, gd), lambda b, g: (0, g)),
            pl.BlockSpec((1, gd), lambda b, g: (0, g)),
        ],
        out_specs=pl.BlockSpec((seq, gd), lambda b, g: (b, g)),
        compiler_params=_params("parallel", "parallel"),
        name="short_conv",
    )(p, p, p, conv_w, out_gain)


def _outproj_kernel(a_ref, c_ref, wa_ref, wc_ref, x_ref, mod_ref, o_ref):
    acc = jnp.dot(a_ref[...], wa_ref[...], preferred_element_type=F32)
    acc = acc + jnp.dot(c_ref[...], wc_ref[...], preferred_element_type=F32)
    o_ref[...] = x_ref[...] + mod_ref[2:3, :] * acc


def _out_projection(attn, conv, w, x, mod, seq):
    t, d = x.shape
    wa = attn.shape[1]
    wc = conv.shape[1]
    assert wa == wc
    tiles_per_seq = seq // ROW_TILE
    return pl.pallas_call(
        _outproj_kernel,
        out_shape=jax.ShapeDtypeStruct((t, d), F32),
        grid=(t // ROW_TILE, d // OUT_TILE),
        in_specs=[
            pl.BlockSpec((ROW_TILE, wa), lambda i, j: (i, 0)),
            pl.BlockSpec((ROW_TILE, wc), lambda i, j: (i, 0)),
            pl.BlockSpec((wa, OUT_TILE), lambda i, j: (0, j)),
            pl.BlockSpec((wc, OUT_TILE), lambda i, j: (1, j)),
            pl.BlockSpec((ROW_TILE, OUT_TILE), lambda i, j: (i, j)),
            pl.BlockSpec((None, 6, OUT_TILE), lambda i, j: (i // tiles_per_seq, 0, j)),
        ],
        out_specs=pl.BlockSpec((ROW_TILE, OUT_TILE), lambda i, j: (i, j)),
        compiler_params=_params("parallel", "parallel"),
        name="out_projection",
    )(attn, conv, w, w, x, mod)


def _swiglu_tile(h, wg, wu, wd, row_scale=None):
    g = jnp.dot(h, wg, preferred_element_type=F32)
    u = jnp.dot(h, wu, preferred_element_type=F32)
    a = g * _sigmoid(g) * u
    if row_scale is not None:
        a = a * row_scale
    return jnp.dot(a.astype(BF16), wd, preferred_element_type=F32)


def _ffn_kernel(x_ref, mod_ref, g_ref, wg_ref, wu_ref, wd_ref, o_ref, h_ref):
    f = pl.program_id(1)

    @pl.when(f == 0)
    def _():
        h = _modulated_norm(x_ref[...], g_ref[...], mod_ref[3:4, :], mod_ref[4:5, :])
        h_ref[...] = h.astype(BF16)
        o_ref[...] = jnp.zeros_like(o_ref)

    o_ref[...] += _swiglu_tile(h_ref[...], wg_ref[...], wu_ref[...], wd_ref[...])

    @pl.when(f == pl.num_programs(1) - 1)
    def _():
        o_ref[...] = x_ref[...] + mod_ref[5:6, :] * o_ref[...]


def _dense_ffn(x, mod, gain, wg, wu, wd, seq):
    t, d = x.shape
    ff = wg.shape[1]
    tiles_per_seq = seq // ROW_TILE
    return pl.pallas_call(
        _ffn_kernel,
        out_shape=jax.ShapeDtypeStruct((t, d), F32),
        grid=(t // ROW_TILE, ff // FF_TILE),
        in_specs=[
            pl.BlockSpec((ROW_TILE, d), lambda i, f: (i, 0), pipeline_mode=pl.Buffered(1)),
            pl.BlockSpec((None, 6, d), lambda i, f: (i // tiles_per_seq, 0, 0)),
            pl.BlockSpec((1, d), lambda i, f: (0, 0)),
            pl.BlockSpec((d, FF_TILE), lambda i, f: (0, f)),
            pl.BlockSpec((d, FF_TILE), lambda i, f: (0, f)),
            pl.BlockSpec((FF_TILE, d), lambda i, f: (f, 0)),
        ],
        out_specs=pl.BlockSpec((ROW_TILE, d), lambda i, f: (i, 0)),
        scratch_shapes=[pltpu.VMEM((ROW_TILE, d), BF16)],
        compiler_params=_params("parallel", "arbitrary"),
        name="dense_ffn",
    )(x, mod, gain, wg, wu, wd)


def _router_kernel(x_ref, mod_ref, g_ref, r_ref, h_ref, idx_ref, wgt_ref):
    h = _modulated_norm(x_ref[...], g_ref[...], mod_ref[3:4, :], mod_ref[4:5, :])
    h_ref[...] = h
    logits = jnp.dot(h, r_ref[...], preferred_element_type=F32, precision=lax.Precision.HIGHEST)
    z = jnp.exp(logits - jnp.max(logits, axis=-1, keepdims=True))
    probs = z / jnp.sum(z, axis=-1, keepdims=True)
    lane = lax.broadcasted_iota(jnp.int32, probs.shape, 1)
    n = probs.shape[1]
    v1 = jnp.max(probs, axis=-1, keepdims=True)
    i1 = jnp.min(jnp.where(probs == v1, lane, n), axis=-1, keepdims=True)
    rest = jnp.where(lane == i1, -1.0, probs)
    v2 = jnp.max(rest, axis=-1, keepdims=True)
    i2 = jnp.min(jnp.where(rest == v2, lane, n), axis=-1, keepdims=True)
    total = v1 + v2
    idx_ref[...] = jnp.where(lane == 0, i1, jnp.where(lane == 1, i2, 0))
    wgt_ref[...] = jnp.where(lane == 0, v1 / total, jnp.where(lane == 1, v2 / total, 0.0))


def _router(x, mod, gain, router, seq):
    t, d = x.shape
    n = router.shape[1]
    tiles_per_seq = seq // ROW_TILE
    return pl.pallas_call(
        _router_kernel,
        out_shape=(jax.ShapeDtypeStruct((t, d), F32),
                   jax.ShapeDtypeStruct((t, n), jnp.int32),
                   jax.ShapeDtypeStruct((t, n), F32)),
        grid=(t // ROW_TILE,),
        in_specs=[
            pl.BlockSpec((ROW_TILE, d), lambda i: (i, 0)),
            pl.BlockSpec((None, 6, d), lambda i: (i // tiles_per_seq, 0, 0)),
            pl.BlockSpec((1, d), lambda i: (0, 0)),
            pl.BlockSpec((d, n), lambda i: (0, 0)),
        ],
        out_specs=(pl.BlockSpec((ROW_TILE, d), lambda i: (i, 0)),
                   pl.BlockSpec((ROW_TILE, n), lambda i: (i, 0)),
                   pl.BlockSpec((ROW_TILE, n), lambda i: (i, 0))),
        compiler_params=_params("parallel"),
        name="router",
    )(x, mod, gain, router)


def _routing_tables(expert_ids, n_experts, max_tiles):
    flat = expert_ids.reshape(-1)
    onehot = (flat[:, None] == jnp.arange(n_experts, dtype=jnp.int32)[None, :]).astype(jnp.int32)
    csum = jnp.cumsum(onehot, axis=0)
    rank = jnp.sum((csum - 1) * onehot, axis=1)
    counts = csum[-1]
    tiles = (counts + MOE_TILE - 1) // MOE_TILE
    tile_end = jnp.cumsum(tiles)
    tile_start = tile_end - tiles
    n_tiles = tile_end[-1]
    pos = tile_start[flat] * MOE_TILE + rank
    tile_ids = jnp.minimum(jnp.arange(max_tiles, dtype=jnp.int32), n_tiles - 1)
    tile_expert = jnp.sum((tile_ids[:, None] >= tile_end[None, :]).astype(jnp.int32), axis=1)
    token = jnp.arange(flat.shape[0], dtype=jnp.int32) // expert_ids.shape[1]
    src_token = jnp.zeros((max_tiles * MOE_TILE,), jnp.int32).at[pos].set(token, unique_indices=True)
    return pos, src_token, tile_expert, n_tiles.reshape(1)


def _row_copy(src_ref, row, dst_ref, slot, sem):
    return pltpu.make_async_copy(src_ref.at[pl.ds(row, 1)], dst_ref.at[pl.ds(slot, 1)], sem)


def _moe_kernel(te_ref, nt_ref, idx0_ref, idxn_ref, h_ref, wg_ref, wu_ref, wd_ref, o_ref,
                xbuf_ref, hb_ref, sem, *, rows_per_step):
    k = pl.program_id(0)
    f = pl.program_id(1)
    n_tiles = nt_ref[0]
    slot = k % 2

    @pl.when((k == 0) & (f == 0))
    def _():
        def issue(r, carry):
            _row_copy(h_ref, idx0_ref[0, 0, r], xbuf_ref.at[0], r, sem.at[0]).start()
            return carry

        lax.fori_loop(0, MOE_TILE, issue, 0, unroll=8)

    @pl.when(f == 0)
    def _():
        o_ref[...] = jnp.zeros_like(o_ref)

        @pl.when(k <= n_tiles)
        def _():
            pltpu.make_async_copy(h_ref.at[pl.ds(0, MOE_TILE)], xbuf_ref.at[slot], sem.at[slot]).wait()
            hb_ref[...] = xbuf_ref[slot].astype(BF16)

    @pl.when(k < n_tiles)
    def _():
        o_ref[...] += _swiglu_tile(hb_ref[...], wg_ref[...], wu_ref[...], wd_ref[...])

        for step in range(MOE_TILE // rows_per_step):
            @pl.when(f == step)
            def _(first=step * rows_per_step):
                for r in range(first, first + rows_per_step):
                    _row_copy(h_ref, idxn_ref[0, 0, r], xbuf_ref.at[1 - slot], r,
                              sem.at[1 - slot]).start()


def _moe_ffn(h, src_token, tile_expert, n_tiles, wg, wu, wd):
    d = h.shape[1]
    ff = wg.shape[2]
    n_f = ff // MOE_FF_TILE
    grid_tiles = tile_expert.shape[0]
    idx = src_token.reshape(grid_tiles, 1, MOE_TILE)

    def col(k, f, nt):
        return jnp.where(k < nt[0], f, n_f - 1)

    return pl.pallas_call(
        functools.partial(_moe_kernel, rows_per_step=MOE_TILE // max(1, n_f // 2)),
        out_shape=jax.ShapeDtypeStruct((grid_tiles * MOE_TILE, d), F32),
        grid_spec=pltpu.PrefetchScalarGridSpec(
            num_scalar_prefetch=2,
            grid=(grid_tiles, n_f),
            in_specs=[
                pl.BlockSpec((1, 1, MOE_TILE), lambda k, f, te, nt: (0, 0, 0), memory_space=pltpu.SMEM),
                pl.BlockSpec((1, 1, MOE_TILE),
                             lambda k, f, te, nt: (jnp.minimum(k + 1, grid_tiles - 1), 0, 0),
                             memory_space=pltpu.SMEM),
                pl.BlockSpec(memory_space=pl.ANY),
                pl.BlockSpec((None, d, MOE_FF_TILE), lambda k, f, te, nt: (te[k], 0, col(k, f, nt))),
                pl.BlockSpec((None, d, MOE_FF_TILE), lambda k, f, te, nt: (te[k], 0, col(k, f, nt))),
                pl.BlockSpec((None, MOE_FF_TILE, d), lambda k, f, te, nt: (te[k], col(k, f, nt), 0)),
            ],
            out_specs=pl.BlockSpec((MOE_TILE, d), lambda k, f, te, nt: (k, 0)),
            scratch_shapes=[pltpu.VMEM((2, MOE_TILE, d), F32), pltpu.VMEM((MOE_TILE, d), BF16),
                            pltpu.SemaphoreType.DMA((2,))],
        ),
        compiler_params=_params("arbitrary", "arbitrary"),
        name="moe_ffn",
    )(tile_expert, n_tiles, idx, idx, h, wg, wu, wd)


def _combine_kernel(pos_ref, y_ref, x_ref, w_ref, mod_ref, o_ref, buf0_ref, buf1_ref, sem):
    rows = buf0_ref.shape[0]

    for r in range(rows):
        _row_copy(y_ref, pos_ref[0, 0, 2 * r], buf0_ref, r, sem).start()
        _row_copy(y_ref, pos_ref[0, 0, 2 * r + 1], buf1_ref, r, sem).start()
    pltpu.make_async_copy(y_ref.at[pl.ds(0, rows)], buf0_ref, sem).wait()
    pltpu.make_async_copy(y_ref.at[pl.ds(0, rows)], buf1_ref, sem).wait()
    w = w_ref[...]
    f = w[:, 0:1] * buf0_ref[...] + w[:, 1:2] * buf1_ref[...]
    o_ref[...] = x_ref[...] + mod_ref[5:6, :] * f


def _combine(y, pos, weights, x, mod, seq):
    t, d = x.shape
    n = weights.shape[1]
    tiles_per_seq = seq // COMBINE_TILE
    return pl.pallas_call(
        _combine_kernel,
        out_shape=jax.ShapeDtypeStruct((t, d), F32),
        grid=(t // COMBINE_TILE,),
        in_specs=[
            pl.BlockSpec((1, 1, 2 * COMBINE_TILE), lambda i: (i, 0, 0), memory_space=pltpu.SMEM),
            pl.BlockSpec(memory_space=pl.ANY),
            pl.BlockSpec((COMBINE_TILE, d), lambda i: (i, 0)),
            pl.BlockSpec((COMBINE_TILE, n), lambda i: (i, 0)),
            pl.BlockSpec((None, 6, d), lambda i: (i // tiles_per_seq, 0, 0)),
        ],
        out_specs=pl.BlockSpec((COMBINE_TILE, d), lambda i: (i, 0)),
        scratch_shapes=[pltpu.VMEM((COMBINE_TILE, d), F32), pltpu.VMEM((COMBINE_TILE, d), F32),
                        pltpu.SemaphoreType.DMA],
        compiler_params=_params("arbitrary"),
        name="moe_combine",
    )(pos.reshape(t // COMBINE_TILE, 1, 2 * COMBINE_TILE), y, x, weights, mod)


def _routed_ffn(x, mod, gain, router, wg, wu, wd, seq):
    t = x.shape[0]
    n_experts = router.shape[1]
    top_k = 2
    grid_tiles = (t * top_k) // MOE_TILE + n_experts + 1
    h, idx, wgt = _router(x, mod, gain, router, seq)
    pos, src_token, tile_expert, n_tiles = _routing_tables(idx[:, :top_k], n_experts, grid_tiles)
    y = _moe_ffn(h, src_token, tile_expert, n_tiles, wg, wu, wd)
    return _combine(y, pos, wgt, x, mod, seq)


def kernel(x, c, norm1_g, w_ada, b_ada, w_in, q_norm_g, k_norm_g, conv_w, attn_out_g, conv_out_g,
           w_out, norm2_g, dense_w_gate, dense_w_up, dense_w_down, moe_router, moe_w_gate,
           moe_w_up, moe_w_down):
    batch, seq, d = x.shape
    depth = w_in.shape[0]
    assert seq % ROW_TILE == 0 and seq % Q_TILE == 0 and seq % COMBINE_TILE == 0
    tables = _rope_tables(seq)
    mod = _ada_mod(c, w_ada, b_ada)
    xt = x.reshape(batch * seq, d)
    for l in range(depth):
        p = _in_projection(xt, mod[l], norm1_g[l][None], w_in[l].astype(BF16), seq)
        attn = _attention(p, tables, q_norm_g[l][None], k_norm_g[l][None], attn_out_g[l][None],
                          batch, seq)
        conv = _short_conv(p, conv_w[l], conv_out_g[l][None], batch, seq)
        xt = _out_projection(attn, conv, w_out[l].astype(BF16), xt, mod[l], seq)
        i = l // 2
        if l % 2 == 0:
            xt = _dense_ffn(xt, mod[l], norm2_g[l][None], dense_w_gate[i].astype(BF16),
                            dense_w_up[i].astype(BF16), dense_w_down[i].astype(BF16), seq)
        else:
            xt = _routed_ffn(xt, mod[l], norm2_g[l][None], moe_router[i],
                             moe_w_gate[i].astype(BF16), moe_w_up[i].astype(BF16),
                             moe_w_down[i].astype(BF16), seq)
    return xt.reshape(batch, seq, d)
```

```python
import functools

import jax
import jax.numpy as jnp
from jax import lax
from jax.experimental import pallas as pl
from jax.experimental.pallas import tpu as pltpu

F32 = jnp.float32
BF16 = jnp.bfloat16

N_HEADS = 8
N_KV_HEADS = 2
HEAD_DIM = 128
GQA = N_HEADS // N_KV_HEADS
ATTN_W = N_HEADS * HEAD_DIM
KV_W = N_KV_HEADS * HEAD_DIM
ROPE_THETA = 10000.0
GRID_W = 64
CONV_GROUPS = 8
N_EXPERTS = 8
EPS = 1e-6

V7X_VMEM_BYTES = 64 * 1024 * 1024
VMEM_LIMIT = V7X_VMEM_BYTES - 8 * 1024 * 1024

ROW_TILE = 1024
Q_TILE = 512
FF_TILE = 512
IN_TILE = 1536
OUT_TILE = 1024
ADA_TILE = 1024
MOE_TILE = 512
MOE_FF_TILE = 1024
COMBINE_TILE = 512
CONV_BLOCK_GROUPS = 4


def _params(*sem):
    return pltpu.CompilerParams(dimension_semantics=sem, vmem_limit_bytes=VMEM_LIMIT)


def _sigmoid(x):
    return 1.0 / (1.0 + jnp.exp(-x))


def _modulated_norm(x, gain, shift, scale):
    ms = jnp.mean(x * x, axis=-1, keepdims=True)
    return (x * lax.rsqrt(ms + EPS) * gain) * (1.0 + scale) + shift


def _ada_kernel(c_ref, w_ref, b_ref, o_ref):
    c = c_ref[...]
    act = (c * _sigmoid(c)).astype(BF16)
    o_ref[...] = jnp.dot(act, w_ref[...].astype(BF16), preferred_element_type=F32) + b_ref[...]


def _ada_mod(c, w_ada, b_ada):
    depth, d, n = w_ada.shape
    b = c.shape[0]
    out = pl.pallas_call(
        _ada_kernel,
        out_shape=jax.ShapeDtypeStruct((depth, b, n), F32),
        grid=(depth, n // ADA_TILE),
        in_specs=[
            pl.BlockSpec((b, d), lambda l, j: (0, 0)),
            pl.BlockSpec((None, d, ADA_TILE), lambda l, j: (l, 0, j)),
            pl.BlockSpec((None, 1, ADA_TILE), lambda l, j: (l, 0, j)),
        ],
        out_specs=pl.BlockSpec((None, b, ADA_TILE), lambda l, j: (l, 0, j)),
        compiler_params=_params("parallel", "parallel"),
        name="ada_mod",
    )(c, w_ada, b_ada.reshape(depth, 1, n))
    return out.reshape(depth, b, 6, d)


def _inproj_kernel(x_ref, mod_ref, g_ref, w_ref, o_ref, h_ref):
    @pl.when(pl.program_id(1) == 0)
    def _():
        h = _modulated_norm(x_ref[...], g_ref[...], mod_ref[0:1, :], mod_ref[1:2, :])
        h_ref[...] = h.astype(BF16)

    o_ref[...] = jnp.dot(h_ref[...], w_ref[...], preferred_element_type=F32).astype(o_ref.dtype)


def _in_projection(x, mod, gain, w, seq):
    t, d = x.shape
    n = w.shape[1]
    tiles_per_seq = seq // ROW_TILE
    return pl.pallas_call(
        _inproj_kernel,
        out_shape=jax.ShapeDtypeStruct((t, n), BF16),
        grid=(t // ROW_TILE, n // IN_TILE),
        in_specs=[
            pl.BlockSpec((ROW_TILE, d), lambda i, j: (i, 0)),
            pl.BlockSpec((None, 6, d), lambda i, j: (i // tiles_per_seq, 0, 0)),
            pl.BlockSpec((1, d), lambda i, j: (0, 0)),
            pl.BlockSpec((d, IN_TILE), lambda i, j: (0, j)),
        ],
        out_specs=pl.BlockSpec((ROW_TILE, IN_TILE), lambda i, j: (i, j)),
        scratch_shapes=[pltpu.VMEM((ROW_TILE, d), BF16)],
        compiler_params=_params("parallel", "arbitrary"),
        name="in_projection",
    )(x, mod, gain, w)


def _rope_tables(seq):
    pos = jnp.arange(seq, dtype=jnp.int32)
    row = (pos // GRID_W).astype(F32)
    col = (pos % GRID_W).astype(F32)
    quarter = HEAD_DIM // 4
    inv = ROPE_THETA ** (-jnp.arange(quarter, dtype=F32) / quarter)
    ang_r = row[:, None] * inv
    ang_c = col[:, None] * inv
    ang = jnp.concatenate([ang_r, ang_r, ang_c, ang_c], axis=-1)
    cos, sin = jnp.cos(ang), jnp.sin(ang)
    first_half = (jnp.arange(HEAD_DIM) % (2 * quarter)) < quarter
    sin_up = jnp.where(first_half, -sin, 0.0)
    sin_dn = jnp.where(first_half, 0.0, sin)
    return cos, sin_up, sin_dn


def _norm_rope(x, gain, cos, sin_up, sin_dn):
    ms = jnp.mean(x * x, axis=-1, keepdims=True)
    y = x * lax.rsqrt(ms + EPS) * gain
    quarter = HEAD_DIM // 4
    return (y * cos + pltpu.roll(y, HEAD_DIM - quarter, 1) * sin_up
            + pltpu.roll(y, quarter, 1) * sin_dn)


def _attn_kernel(q_ref, k_ref, v_ref, cos_ref, su_ref, sd_ref, qg_ref, kg_ref, og_ref,
                 o_ref, kt_ref):
    qt = pl.program_id(2)

    @pl.when(qt == 0)
    def _():
        k = _norm_rope(k_ref[...].astype(F32), kg_ref[...], cos_ref[...], su_ref[...], sd_ref[...])
        kt_ref[...] = k.T.astype(BF16)

    r0 = pl.multiple_of(qt * Q_TILE, Q_TILE)
    cos = cos_ref[pl.ds(r0, Q_TILE), :]
    s_up = su_ref[pl.ds(r0, Q_TILE), :]
    s_dn = sd_ref[pl.ds(r0, Q_TILE), :]
    scale = HEAD_DIM ** -0.5
    for g in range(GQA):
        cols = slice(g * HEAD_DIM, (g + 1) * HEAD_DIM)
        q = _norm_rope(q_ref[:, cols].astype(F32), qg_ref[...], cos, s_up, s_dn) * scale
        s = jnp.dot(q.astype(BF16), kt_ref[...], preferred_element_type=F32)
        p = jnp.exp(s - jnp.max(s, axis=-1, keepdims=True))
        denom = jnp.sum(p, axis=-1, keepdims=True)
        o = jnp.dot(p.astype(BF16), v_ref[...], preferred_element_type=F32) / denom
        ms = jnp.mean(o * o, axis=-1, keepdims=True)
        o_ref[:, cols] = (o * lax.rsqrt(ms + EPS) * og_ref[:, cols]).astype(o_ref.dtype)


def _attention(p, tables, q_gain, k_gain, out_gain, batch, seq):
    t = p.shape[0]
    q_tiles = seq // Q_TILE
    group_w = GQA * HEAD_DIM
    k_blk0 = ATTN_W // HEAD_DIM
    v_blk0 = (ATTN_W + KV_W) // HEAD_DIM
    table_spec = pl.BlockSpec((seq, HEAD_DIM), lambda b, h, i: (0, 0))
    gain_spec = pl.BlockSpec((1, HEAD_DIM), lambda b, h, i: (0, 0))
    return pl.pallas_call(
        _attn_kernel,
        out_shape=jax.ShapeDtypeStruct((t, ATTN_W), BF16),
        grid=(batch, N_KV_HEADS, q_tiles),
        in_specs=[
            pl.BlockSpec((Q_TILE, group_w), lambda b, h, i: (b * q_tiles + i, h)),
            pl.BlockSpec((seq, HEAD_DIM), lambda b, h, i: (b, k_blk0 + h)),
            pl.BlockSpec((seq, HEAD_DIM), lambda b, h, i: (b, v_blk0 + h)),
            table_spec, table_spec, table_spec,
            gain_spec, gain_spec,
            pl.BlockSpec((1, group_w), lambda b, h, i: (0, h)),
        ],
        out_specs=pl.BlockSpec((Q_TILE, group_w), lambda b, h, i: (b * q_tiles + i, h)),
        scratch_shapes=[pltpu.VMEM((HEAD_DIM, seq), BF16)],
        compiler_params=_params("parallel", "parallel", "arbitrary"),
        name="attention",
    )(p, p, p, *tables, q_gain, k_gain, out_gain)


def _conv_kernel(h_ref, b_ref, c_ref, w_ref, g_ref, o_ref, *, group_dim):
    seq, width = o_ref.shape
    pos = lax.broadcasted_iota(jnp.int32, (seq, group_dim), 0)
    for j in range(width // group_dim):
        cols = slice(j * group_dim, (j + 1) * group_dim)
        u = c_ref[:, cols].astype(F32) * h_ref[:, cols].astype(F32)
        prev = jnp.where(pos == 0, 0.0, pltpu.roll(u, 1, 0))
        nxt = jnp.where(pos == seq - 1, 0.0, pltpu.roll(u, seq - 1, 0))
        y = prev * w_ref[0:1, cols] + u * w_ref[1:2, cols] + nxt * w_ref[2:3, cols]
        y = b_ref[:, cols].astype(F32) * y
        ms = jnp.mean(y * y, axis=-1, keepdims=True)
        o_ref[:, cols] = (y * lax.rsqrt(ms + EPS) * g_ref[:, cols]).astype(o_ref.dtype)


def _short_conv(p, conv_w, out_gain, batch, seq):
    t = p.shape[0]
    conv_w_total = conv_w.shape[1]
    gd = conv_w_total // CONV_GROUPS
    bw = CONV_BLOCK_GROUPS * gd
    n_blocks = conv_w_total // bw
    h0 = (ATTN_W + 2 * KV_W) // bw
    b0 = h0 + n_blocks
    c0 = b0 + n_blocks
    return pl.pallas_call(
        functools.partial(_conv_kernel, group_dim=gd),
        out_shape=jax.ShapeDtypeStruct((t, conv_w_total), BF16),
        grid=(batch, n_blocks),
        in_specs=[
            pl.BlockSpec((seq, bw), lambda b, g: (b, h0 + g)),
            pl.BlockSpec((seq, bw), lambda b, g: (b, b0 + g)),
            pl.BlockSpec((seq, bw), lambda b, g: (b, c0 + g)),
            pl.BlockSpec((3, bw), lambda b, g: (0, g)),
            pl.BlockSpec((1, bw), lambda b, g: (0, g)),
        ],
        out_specs=pl.BlockSpec((seq, bw), lambda b, g: (b, g)),
        compiler_params=_params("parallel", "parallel"),
        name="short_conv",
    )(p, p, p, conv_w, out_gain)


def _outproj_kernel(a_ref, c_ref, wa_ref, wc_ref, x_ref, mod_ref, o_ref):
    acc = jnp.dot(a_ref[...], wa_ref[...], preferred_element_type=F32)
    acc = acc + jnp.dot(c_ref[...], wc_ref[...], preferred_element_type=F32)
    o_ref[...] = x_ref[...] + mod_ref[2:3, :] * acc


def _out_projection(attn, conv, w, x, mod, seq):
    t, d = x.shape
    wa = attn.shape[1]
    wc = conv.shape[1]
    assert wa == wc
    tiles_per_seq = seq // ROW_TILE
    return pl.pallas_call(
        _outproj_kernel,
        out_shape=jax.ShapeDtypeStruct((t, d), F32),
        grid=(t // ROW_TILE, d // OUT_TILE),
        in_specs=[
            pl.BlockSpec((ROW_TILE, wa), lambda i, j: (i, 0)),
            pl.BlockSpec((ROW_TILE, wc), lambda i, j: (i, 0)),
            pl.BlockSpec((wa, OUT_TILE), lambda i, j: (0, j)),
            pl.BlockSpec((wc, OUT_TILE), lambda i, j: (1, j)),
            pl.BlockSpec((ROW_TILE, OUT_TILE), lambda i, j: (i, j)),
            pl.BlockSpec((None, 6, OUT_TILE), lambda i, j: (i // tiles_per_seq, 0, j)),
        ],
        out_specs=pl.BlockSpec((ROW_TILE, OUT_TILE), lambda i, j: (i, j)),
        compiler_params=_params("parallel", "parallel"),
        name="out_projection",
    )(attn, conv, w, w, x, mod)


def _swiglu_tile(h, wg, wu, wd, row_scale=None):
    g = jnp.dot(h, wg, preferred_element_type=F32)
    u = jnp.dot(h, wu, preferred_element_type=F32)
    a = g * _sigmoid(g) * u
    if row_scale is not None:
        a = a * row_scale
    return jnp.dot(a.astype(BF16), wd, preferred_element_type=F32)


def _ffn_kernel(x_ref, mod_ref, g_ref, wg_ref, wu_ref, wd_ref, o_ref, h_ref):
    f = pl.program_id(1)

    @pl.when(f == 0)
    def _():
        h = _modulated_norm(x_ref[...], g_ref[...], mod_ref[3:4, :], mod_ref[4:5, :])
        h_ref[...] = h.astype(BF16)
        o_ref[...] = jnp.zeros_like(o_ref)

    o_ref[...] += _swiglu_tile(h_ref[...], wg_ref[...], wu_ref[...], wd_ref[...])

    @pl.when(f == pl.num_programs(1) - 1)
    def _():
        o_ref[...] = x_ref[...] + mod_ref[5:6, :] * o_ref[...]


def _dense_ffn(x, mod, gain, wg, wu, wd, seq):
    t, d = x.shape
    ff = wg.shape[1]
    tiles_per_seq = seq // ROW_TILE
    return pl.pallas_call(
        _ffn_kernel,
        out_shape=jax.ShapeDtypeStruct((t, d), F32),
        grid=(t // ROW_TILE, ff // FF_TILE),
        in_specs=[
            pl.BlockSpec((ROW_TILE, d), lambda i, f: (i, 0), pipeline_mode=pl.Buffered(1)),
            pl.BlockSpec((None, 6, d), lambda i, f: (i // tiles_per_seq, 0, 0)),
            pl.BlockSpec((1, d), lambda i, f: (0, 0)),
            pl.BlockSpec((d, FF_TILE), lambda i, f: (0, f)),
            pl.BlockSpec((d, FF_TILE), lambda i, f: (0, f)),
            pl.BlockSpec((FF_TILE, d), lambda i, f: (f, 0)),
        ],
        out_specs=pl.BlockSpec((ROW_TILE, d), lambda i, f: (i, 0)),
        scratch_shapes=[pltpu.VMEM((ROW_TILE, d), BF16)],
        compiler_params=_params("parallel", "arbitrary"),
        name="dense_ffn",
    )(x, mod, gain, wg, wu, wd)


def _router_kernel(x_ref, mod_ref, g_ref, r_ref, h_ref, idx_ref, wgt_ref):
    h = _modulated_norm(x_ref[...], g_ref[...], mod_ref[3:4, :], mod_ref[4:5, :])
    h_ref[...] = h
    n = idx_ref.shape[1]
    h_hi = h.astype(BF16)
    h_lo = (h - h_hi.astype(F32)).astype(BF16)
    top = jnp.dot(h_hi, r_ref[...], preferred_element_type=F32)
    low = jnp.dot(h_lo, r_ref[...], preferred_element_type=F32)
    logits = (top[:, :n] + top[:, n:]) + (low[:, :n] + low[:, n:])
    z = jnp.exp(logits - jnp.max(logits, axis=-1, keepdims=True))
    probs = z / jnp.sum(z, axis=-1, keepdims=True)
    lane = lax.broadcasted_iota(jnp.int32, probs.shape, 1)
    n = probs.shape[1]
    v1 = jnp.max(probs, axis=-1, keepdims=True)
    i1 = jnp.min(jnp.where(probs == v1, lane, n), axis=-1, keepdims=True)
    rest = jnp.where(lane == i1, -1.0, probs)
    v2 = jnp.max(rest, axis=-1, keepdims=True)
    i2 = jnp.min(jnp.where(rest == v2, lane, n), axis=-1, keepdims=True)
    total = v1 + v2
    idx_ref[...] = jnp.where(lane == 0, i1, jnp.where(lane == 1, i2, 0))
    wgt_ref[...] = jnp.where(lane == 0, v1 / total, jnp.where(lane == 1, v2 / total, 0.0))


def _router(x, mod, gain, router, seq):
    t, d = x.shape
    n = router.shape[1]
    tiles_per_seq = seq // ROW_TILE
    r_hi = router.astype(BF16)
    r_lo = (router - r_hi.astype(F32)).astype(BF16)
    r_split = jnp.concatenate([r_hi, r_lo], axis=1)
    return pl.pallas_call(
        _router_kernel,
        out_shape=(jax.ShapeDtypeStruct((t, d), F32),
                   jax.ShapeDtypeStruct((t, n), jnp.int32),
                   jax.ShapeDtypeStruct((t, n), F32)),
        grid=(t // ROW_TILE,),
        in_specs=[
            pl.BlockSpec((ROW_TILE, d), lambda i: (i, 0)),
            pl.BlockSpec((None, 6, d), lambda i: (i // tiles_per_seq, 0, 0)),
            pl.BlockSpec((1, d), lambda i: (0, 0)),
            pl.BlockSpec((d, 2 * n), lambda i: (0, 0)),
        ],
        out_specs=(pl.BlockSpec((ROW_TILE, d), lambda i: (i, 0)),
                   pl.BlockSpec((ROW_TILE, n), lambda i: (i, 0)),
                   pl.BlockSpec((ROW_TILE, n), lambda i: (i, 0))),
        compiler_params=_params("parallel"),
        name="router",
    )(x, mod, gain, r_split)


def _routing_tables(expert_ids, n_experts, max_tiles):
    flat = expert_ids.reshape(-1)
    onehot = (flat[:, None] == jnp.arange(n_experts, dtype=jnp.int32)[None, :]).astype(jnp.int32)
    csum = jnp.cumsum(onehot, axis=0)
    rank = jnp.sum((csum - 1) * onehot, axis=1)
    counts = csum[-1]
    tiles = (counts + MOE_TILE - 1) // MOE_TILE
    tile_end = jnp.cumsum(tiles)
    tile_start = tile_end - tiles
    n_tiles = tile_end[-1]
    pos = tile_start[flat] * MOE_TILE + rank
    tile_ids = jnp.minimum(jnp.arange(max_tiles, dtype=jnp.int32), n_tiles - 1)
    tile_expert = jnp.sum((tile_ids[:, None] >= tile_end[None, :]).astype(jnp.int32), axis=1)
    token = jnp.arange(flat.shape[0], dtype=jnp.int32) // expert_ids.shape[1]
    src_token = jnp.zeros((max_tiles * MOE_TILE,), jnp.int32).at[pos].set(token, unique_indices=True)
    return pos, src_token, tile_expert, n_tiles.reshape(1)


def _row_copy(src_ref, row, dst_ref, slot, sem):
    return pltpu.make_async_copy(src_ref.at[pl.ds(row, 1)], dst_ref.at[pl.ds(slot, 1)], sem)


def _moe_kernel(te_ref, nt_ref, idx0_ref, idxn_ref, h_ref, wg_ref, wu_ref, wd_ref, o_ref,
                xbuf_ref, hb_ref, sem, *, rows_per_step):
    k = pl.program_id(0)
    f = pl.program_id(1)
    n_tiles = nt_ref[0]
    slot = k % 2

    @pl.when((k == 0) & (f == 0))
    def _():
        def issue(r, carry):
            _row_copy(h_ref, idx0_ref[0, 0, r], xbuf_ref.at[0], r, sem.at[0]).start()
            return carry

        lax.fori_loop(0, MOE_TILE, issue, 0, unroll=8)

    @pl.when(f == 0)
    def _():
        o_ref[...] = jnp.zeros_like(o_ref)

        @pl.when(k <= n_tiles)
        def _():
            pltpu.make_async_copy(h_ref.at[pl.ds(0, MOE_TILE)], xbuf_ref.at[slot], sem.at[slot]).wait()
            hb_ref[...] = xbuf_ref[slot].astype(BF16)

    @pl.when(k < n_tiles)
    def _():
        o_ref[...] += _swiglu_tile(hb_ref[...], wg_ref[...], wu_ref[...], wd_ref[...])

        for step in range(MOE_TILE // rows_per_step):
            @pl.when(f == step)
            def _(first=step * rows_per_step):
                for r in range(first, first + rows_per_step):
                    _row_copy(h_ref, idxn_ref[0, 0, r], xbuf_ref.at[1 - slot], r,
                              sem.at[1 - slot]).start()


def _moe_ffn(h, src_token, tile_expert, n_tiles, wg, wu, wd):
    d = h.shape[1]
    ff = wg.shape[2]
    n_f = ff // MOE_FF_TILE
    grid_tiles = tile_expert.shape[0]
    idx = src_token.reshape(grid_tiles, 1, MOE_TILE)

    def col(k, f, nt):
        return jnp.where(k < nt[0], f, n_f - 1)

    return pl.pallas_call(
        functools.partial(_moe_kernel, rows_per_step=MOE_TILE // max(1, n_f // 2)),
        out_shape=jax.ShapeDtypeStruct((grid_tiles * MOE_TILE, d), F32),
        grid_spec=pltpu.PrefetchScalarGridSpec(
            num_scalar_prefetch=2,
            grid=(grid_tiles, n_f),
            in_specs=[
                pl.BlockSpec((1, 1, MOE_TILE), lambda k, f, te, nt: (0, 0, 0), memory_space=pltpu.SMEM),
                pl.BlockSpec((1, 1, MOE_TILE),
                             lambda k, f, te, nt: (jnp.minimum(k + 1, grid_tiles - 1), 0, 0),
                             memory_space=pltpu.SMEM),
                pl.BlockSpec(memory_space=pl.ANY),
                pl.BlockSpec((None, d, MOE_FF_TILE), lambda k, f, te, nt: (te[k], 0, col(k, f, nt))),
                pl.BlockSpec((None, d, MOE_FF_TILE), lambda k, f, te, nt: (te[k], 0, col(k, f, nt))),
                pl.BlockSpec((None, MOE_FF_TILE, d), lambda k, f, te, nt: (te[k], col(k, f, nt), 0)),
            ],
            out_specs=pl.BlockSpec((MOE_TILE, d), lambda k, f, te, nt: (k, 0)),
            scratch_shapes=[pltpu.VMEM((2, MOE_TILE, d), F32), pltpu.VMEM((MOE_TILE, d), BF16),
                            pltpu.SemaphoreType.DMA((2,))],
        ),
        compiler_params=_params("arbitrary", "arbitrary"),
        name="moe_ffn",
    )(tile_expert, n_tiles, idx, idx, h, wg, wu, wd)


def _combine_kernel(pos_ref, y_ref, x_ref, w_ref, mod_ref, o_ref, buf0_ref, buf1_ref, sem):
    rows = buf0_ref.shape[0]

    for r in range(rows):
        _row_copy(y_ref, pos_ref[0, 0, 2 * r], buf0_ref, r, sem).start()
        _row_copy(y_ref, pos_ref[0, 0, 2 * r + 1], buf1_ref, r, sem).start()
    pltpu.make_async_copy(y_ref.at[pl.ds(0, rows)], buf0_ref, sem).wait()
    pltpu.make_async_copy(y_ref.at[pl.ds(0, rows)], buf1_ref, sem).wait()
    w = w_ref[...]
    f = w[:, 0:1] * buf0_ref[...] + w[:, 1:2] * buf1_ref[...]
    o_ref[...] = x_ref[...] + mod_ref[5:6, :] * f


def _combine(y, pos, weights, x, mod, seq):
    t, d = x.shape
    n = weights.shape[1]
    tiles_per_seq = seq // COMBINE_TILE
    return pl.pallas_call(
        _combine_kernel,
        out_shape=jax.ShapeDtypeStruct((t, d), F32),
        grid=(t // COMBINE_TILE,),
        in_specs=[
            pl.BlockSpec((1, 1, 2 * COMBINE_TILE), lambda i: (i, 0, 0), memory_space=pltpu.SMEM),
            pl.BlockSpec(memory_space=pl.ANY),
            pl.BlockSpec((COMBINE_TILE, d), lambda i: (i, 0)),
            pl.BlockSpec((COMBINE_TILE, n), lambda i: (i, 0)),
            pl.BlockSpec((None, 6, d), lambda i: (i // tiles_per_seq, 0, 0)),
        ],
        out_specs=pl.BlockSpec((COMBINE_TILE, d), lambda i: (i, 0)),
        scratch_shapes=[pltpu.VMEM((COMBINE_TILE, d), F32), pltpu.VMEM((COMBINE_TILE, d), F32),
                        pltpu.SemaphoreType.DMA],
        compiler_params=_params("arbitrary"),
        name="moe_combine",
    )(pos.reshape(t // COMBINE_TILE, 1, 2 * COMBINE_TILE), y, x, weights, mod)


def _routed_ffn(x, mod, gain, router, wg, wu, wd, seq):
    t = x.shape[0]
    n_experts = router.shape[1]
    top_k = 2
    grid_tiles = (t * top_k) // MOE_TILE + n_experts + 1
    h, idx, wgt = _router(x, mod, gain, router, seq)
    pos, src_token, tile_expert, n_tiles = _routing_tables(idx[:, :top_k], n_experts, grid_tiles)
    y = _moe_ffn(h, src_token, tile_expert, n_tiles, wg, wu, wd)
    return _combine(y, pos, wgt, x, mod, seq)


def kernel(x, c, norm1_g, w_ada, b_ada, w_in, q_norm_g, k_norm_g, conv_w, attn_out_g, conv_out_g,
           w_out, norm2_g, dense_w_gate, dense_w_up, dense_w_down, moe_router, moe_w_gate,
           moe_w_up, moe_w_down):
    batch, seq, d = x.shape
    depth = w_in.shape[0]
    assert seq % ROW_TILE == 0 and seq % Q_TILE == 0 and seq % COMBINE_TILE == 0
    tables = _rope_tables(seq)
    mod = _ada_mod(c, w_ada, b_ada)
    xt = x.reshape(batch * seq, d)
    for l in range(depth):
        p = _in_projection(xt, mod[l], norm1_g[l][None], w_in[l].astype(BF16), seq)
        attn = _attention(p, tables, q_norm_g[l][None], k_norm_g[l][None], attn_out_g[l][None],
                          batch, seq)
        conv = _short_conv(p, conv_w[l], conv_out_g[l][None], batch, seq)
        xt = _out_projection(attn, conv, w_out[l].astype(BF16), xt, mod[l], seq)
        i = l // 2
        if l % 2 == 0:
            xt = _dense_ffn(xt, mod[l], norm2_g[l][None], dense_w_gate[i].astype(BF16),
                            dense_w_up[i].astype(BF16), dense_w_down[i].astype(BF16), seq)
        else:
            xt = _routed_ffn(xt, mod[l], norm2_g[l][None], moe_router[i],
                             moe_w_gate[i].astype(BF16), moe_w_up[i].astype(BF16),
                             moe_w_down[i].astype(BF16), seq)
    return xt.reshape(batch, seq, d)
```
